```python
import functools
import jax, jax.numpy as jnp
from jax import lax
import numpy as np

D_MODEL = 4096
BATCH = 4
SEQ = 2048
DEPTH = 1
DEC_BATCH = 128
DEC_SEQ = 8
PAST_LEN = 16384
PAGE_SIZE = 128

A_HEADS = 16
A_NOPE = 128
A_ROPE = 64
A_VDIM = 128
Q_LORA = 1024
KV_LORA = 512
ROPE_THETA = 10000.0
A_SCALE = (A_NOPE + A_ROPE) ** -0.5
Q_BLOCK = 128
M_HEADS = 4
M_DQK = 256
M_DV = 512
M_CHUNK = 64
D_FF = 11008
CONV_W = 3
PLE_DIM = 256
EPS = 1e-6

SPLITS = (Q_LORA, KV_LORA, A_ROPE, M_HEADS * M_DQK, M_HEADS * M_DQK, M_HEADS * M_DV,
          M_HEADS, M_HEADS, M_HEADS * M_DV, D_MODEL, D_MODEL)

kernel_name = 'mla_mlstm_gated_hybrid_step'


def rmsnorm(x, g):
    xf = x.astype(jnp.float32)
    y = xf * lax.rsqrt(jnp.mean(xf * xf, axis=-1, keepdims=True) + EPS)
    return (y * g.astype(jnp.float32)).astype(x.dtype)


def rope(x, pos):
    half = x.shape[-1] // 2
    inv = jnp.power(ROPE_THETA, -jnp.arange(half, dtype=jnp.float32) / half)
    ang = pos[:, None] * inv[None, :]
    cos = jnp.cos(ang)[None, :, None, :].astype(x.dtype)
    sin = jnp.sin(ang)[None, :, None, :].astype(x.dtype)
    x1, x2 = x[..., :half], x[..., half:]
    return jnp.concatenate([x1 * cos - x2 * sin, x1 * sin + x2 * cos], axis=-1)


def mla_attend(q_lat, q_rope, kc, kr, q_pos, k_pos):
    s = jnp.einsum('bqhc,btc->bhqt', q_lat, kc) + jnp.einsum('bqhr,btr->bhqt', q_rope, kr)
    s = s.astype(jnp.float32) * A_SCALE
    mask = k_pos[None, :] <= q_pos[:, None]
    s = jnp.where(mask[None, None], s, -jnp.inf)
    p = jax.nn.softmax(s, axis=-1).astype(kc.dtype)
    return jnp.einsum('bhqt,btc->bqhc', p, kc)


def attend_prompt(q_lat, q_rope, c_kv, k_rope, pos):
    B, S, H, C = q_lat.shape
    nb = S // Q_BLOCK
    def blocks(t):
        return jnp.moveaxis(t.reshape(B, nb, Q_BLOCK, *t.shape[2:]), 1, 0)
    def one(args):
        ql, qr, qp = args
        return mla_attend(ql, qr, c_kv, k_rope, qp, pos)
    out = lax.map(one, (blocks(q_lat), blocks(q_rope), pos.reshape(nb, Q_BLOCK)))
    return jnp.moveaxis(out, 0, 1).reshape(B, S, H, C)


def attend_paged(q_lat, q_rope, c_kv, k_rope, pos, pool_ckv, pool_kr, page_table):
    past = page_table.shape[1] * pool_ckv.shape[1]
    k_pos = jnp.concatenate([jnp.arange(past, dtype=jnp.float32), pos])
    def one(args):
        ql, qr, pt, ckv_new, kr_new = args
        kc = jnp.concatenate([pool_ckv[pt].reshape(past, -1), ckv_new], axis=0)
        kr = jnp.concatenate([pool_kr[pt].reshape(past, -1), kr_new], axis=0)
        return mla_attend(ql[None], qr[None], kc[None], kr[None], pos, k_pos)[0]
    return lax.map(one, (q_lat, q_rope, page_table, c_kv, k_rope))


def mlstm_chunkwise(q, k, v, i_pre, logf, C0, n0, m0):
    B, S, H, DK = q.shape
    L = M_CHUNK if S % M_CHUNK == 0 else S
    nc = S // L
    def chunks(t):
        return jnp.moveaxis(t.astype(jnp.float32).reshape(B, nc, L, *t.shape[2:]), 1, 0)
    causal = jnp.tril(jnp.ones((L, L), dtype=bool))
    def step(carry, xs):
        C, n, m = carry
        qc, kc, vc, ic, fc = xs
        b = jnp.cumsum(fc, axis=1)
        a = b + m[:, None, :]
        D = b[:, :, None, :] - b[:, None, :, :] + ic[:, None, :, :]
        D = jnp.where(causal[None, :, :, None], D, -jnp.inf)
        mt = jnp.maximum(a, jnp.max(D, axis=2))
        w_inter = jnp.exp(a - mt)
        qk = jnp.einsum('bthd,bshd->btsh', qc, kc) * jnp.exp(D - mt[:, :, None, :])
        num = jnp.einsum('btsh,bshe->bthe', qk, vc) + w_inter[..., None] * jnp.einsum('bthd,bhde->bthe', qc, C)
        den = jnp.sum(qk, axis=2) + w_inter * jnp.einsum('bthd,bhd->bth', qc, n)
        h = num / jnp.maximum(jnp.abs(den), jnp.exp(-mt))[..., None]
        bL = b[:, -1]
        g = bL[:, None, :] - b + ic
        m_new = jnp.maximum(bL + m, jnp.max(g, axis=1))
        decay = jnp.exp(bL + m - m_new)
        ws = jnp.exp(g - m_new[:, None, :])
        C_new = decay[..., None, None] * C + jnp.einsum('bsh,bshd,bshe->bhde', ws, kc, vc)
        n_new = decay[..., None] * n + jnp.einsum('bsh,bshd->bhd', ws, kc)
        return (C_new, n_new, m_new), h
    init = (C0.astype(jnp.float32), n0.astype(jnp.float32), m0.astype(jnp.float32))
    (C1, n1, m1), hs = lax.scan(step, init, (chunks(q), chunks(k), chunks(v), chunks(i_pre), chunks(logf)))
    h = jnp.moveaxis(hs, 0, 1).reshape(B, S, H, v.shape[-1])
    return h, C1, n1, m1


def hybrid_layer(x, pe, pos, attend, C0, n0, m0, conv0, w):
    B, S, _ = x.shape
    h = rmsnorm(x, w['g_mix'])
    proj = h @ w['w_in']
    offsets = [int(o) for o in np.cumsum(SPLITS)[:-1]]
    cq, ckv, kr, mq, mk, mv, mi, mf, mo, ga, gb = jnp.split(proj, offsets, axis=-1)
    q = jnp.einsum('bsc,chd->bshd', rmsnorm(cq, w['g_qa']), w['w_uq'])
    q_rope = rope(q[..., A_NOPE:], pos)
    q_lat = jnp.einsum('bshd,chd->bshc', q[..., :A_NOPE], w['w_uk'])
    c_kv = rmsnorm(ckv, w['g_kva'])
    k_rope = rope(kr[:, :, None, :], pos)[:, :, 0, :]
    o_lat = attend(q_lat, q_rope, c_kv, k_rope)
    a_out = jnp.einsum('bshc,che->bshe', o_lat, w['w_uv']).reshape(B, S, A_HEADS * A_VDIM)
    qm = mq.reshape(B, S, M_HEADS, M_DQK) * (M_DQK ** -0.5)
    km = mk.reshape(B, S, M_HEADS, M_DQK)
    vm = mv.reshape(B, S, M_HEADS, M_DV)
    i_pre = mi.astype(jnp.float32) + w['b_i'].astype(jnp.float32)
    logf = jax.nn.log_sigmoid(mf.astype(jnp.float32) + w['b_f'].astype(jnp.float32))
    hm, C1, n1, m1 = mlstm_chunkwise(qm, km, vm, i_pre, logf, C0, n0, m0)
    hm = rmsnorm(hm.astype(x.dtype), w['g_mh']).reshape(B, S, M_HEADS * M_DV)
    b_out = jax.nn.sigmoid(mo) * hm
    merged = jax.nn.sigmoid(ga) * (a_out @ w['w_br_a']) + jax.nn.sigmoid(gb) * (b_out @ w['w_br_b'])
    x = x + merged @ w['w_o']
    u = rmsnorm(x, w['g_ffn']) @ w['w_up']
    gate_pre, val = jnp.split(u, 2, axis=-1)
    padded = jnp.concatenate([conv0.astype(gate_pre.dtype), gate_pre], axis=1)
    conv = w['b_conv']
    for j in range(CONV_W):
        conv = conv + w['w_conv'][j] * padded[:, j:j + S]
    x = x + (jax.nn.gelu(conv) * val) @ w['w_down']
    x = x + jax.nn.sigmoid(rmsnorm(x, w['g_ple']) @ w['w_ple_gate']) * (pe @ w['w_ple_proj'])
    return x, c_kv, k_rope, C1, n1, m1, padded[:, S:]


def setup_inputs(seed: int = 0) -> dict:
    key = jax.random.key(seed)
    ks = iter(list(jax.random.split(key, 48)))
    def nrm(shape, scale=1.0):
        return jax.random.normal(next(ks), shape, jnp.float32) * scale
    def gain(shape):
        return 1.0 + nrm(shape, 0.05)
    n_pages = PAST_LEN // PAGE_SIZE
    n_used = DEC_BATCH * n_pages
    n_pool = n_used + n_used // 4 + 1
    page_table = jax.random.permutation(next(ks), n_pool)[:n_used].reshape(DEC_BATCH, n_pages).astype(jnp.int32)
    d_in = sum(SPLITS)
    return {
        'x_prompt': nrm((BATCH, SEQ, D_MODEL)),
        'x_sample': nrm((DEC_BATCH, DEC_SEQ, D_MODEL)),
        'cache_ckv': nrm((DEPTH, n_pool, PAGE_SIZE, KV_LORA)),
        'cache_krope': nrm((DEPTH, n_pool, PAGE_SIZE, A_ROPE)),
        'state_C': nrm((DEPTH, DEC_BATCH, M_HEADS, M_DQK, M_DV)),
        'state_n': nrm((DEPTH, DEC_BATCH, M_HEADS, M_DQK)),
        'state_m': nrm((DEPTH, DEC_BATCH, M_HEADS), 0.5),
        'state_conv': nrm((DEPTH, DEC_BATCH, CONV_W - 1, D_FF)),
        'page_table': page_table,
        'p_prompt': nrm((DEPTH, BATCH, SEQ, PLE_DIM)),
        'p_sample': nrm((DEPTH, DEC_BATCH, DEC_SEQ, PLE_DIM)),
        'g_mix': gain((DEPTH, D_MODEL)),
        'w_in': nrm((DEPTH, D_MODEL, d_in), D_MODEL ** -0.5),
        'b_i': nrm((DEPTH, M_HEADS), 0.1),
        'b_f': 3.0 + nrm((DEPTH, M_HEADS), 0.5),
        'g_qa': gain((DEPTH, Q_LORA)),
        'w_uq': nrm((DEPTH, Q_LORA, A_HEADS, A_NOPE + A_ROPE), Q_LORA ** -0.5),
        'g_kva': gain((DEPTH, KV_LORA)),
        'w_uk': nrm((DEPTH, KV_LORA, A_HEADS, A_NOPE), KV_LORA ** -0.5),
        'w_uv': nrm((DEPTH, KV_LORA, A_HEADS, A_VDIM), KV_LORA ** -0.5),
        'g_mh': gain((DEPTH, M_HEADS, M_DV)),
        'w_br_a': nrm((DEPTH, A_HEADS * A_VDIM, D_MODEL), (A_HEADS * A_VDIM) ** -0.5),
        'w_br_b': nrm((DEPTH, M_HEADS * M_DV, D_MODEL), (M_HEADS * M_DV) ** -0.5),
        'w_o': nrm((DEPTH, D_MODEL, D_MODEL), D_MODEL ** -0.5),
        'g_ffn': gain((DEPTH, D_MODEL)),
        'w_up': nrm((DEPTH, D_MODEL, 2 * D_FF), D_MODEL ** -0.5),
        'w_conv': nrm((DEPTH, CONV_W, D_FF), CONV_W ** -0.5),
        'b_conv': nrm((DEPTH, D_FF), 0.01),
        'w_down': nrm((DEPTH, D_FF, D_MODEL), D_FF ** -0.5),
        'g_ple': gain((DEPTH, D_MODEL)),
        'w_ple_gate': nrm((DEPTH, D_MODEL, D_MODEL), D_MODEL ** -0.5),
        'w_ple_proj': nrm((DEPTH, PLE_DIM, D_MODEL), PLE_DIM ** -0.5),
        'g_final': gain((D_MODEL,)),
    }


def reference(x_prompt, x_sample, cache_ckv, cache_krope, state_C, state_n, state_m, state_conv,
              page_table, p_prompt, p_sample, g_mix, w_in, b_i, b_f, g_qa, w_uq, g_kva, w_uk, w_uv,
              g_mh, w_br_a, w_br_b, w_o, g_ffn, w_up, w_conv, b_conv, w_down, g_ple, w_ple_gate,
              w_ple_proj, g_final):
    bp, sp = x_prompt.shape[0], x_prompt.shape[1]
    past = page_table.shape[1] * cache_ckv.shape[2]
    pos_p = jnp.arange(sp, dtype=jnp.float32)
    pos_s = past + jnp.arange(x_sample.shape[1], dtype=jnp.float32)
    C0p = jnp.zeros((bp, M_HEADS, M_DQK, M_DV), jnp.float32)
    n0p = jnp.zeros((bp, M_HEADS, M_DQK), jnp.float32)
    m0p = jnp.zeros((bp, M_HEADS), jnp.float32)
    conv0p = jnp.zeros((bp, CONV_W - 1, D_FF), x_prompt.dtype)
    xp, xs = x_prompt, x_sample
    outs_p = [[] for _ in range(6)]
    outs_s = [[] for _ in range(6)]
    for l in range(DEPTH):
        w = {'g_mix': g_mix[l], 'w_in': w_in[l], 'b_i': b_i[l], 'b_f': b_f[l], 'g_qa': g_qa[l],
             'w_uq': w_uq[l], 'g_kva': g_kva[l], 'w_uk': w_uk[l], 'w_uv': w_uv[l], 'g_mh': g_mh[l],
             'w_br_a': w_br_a[l], 'w_br_b': w_br_b[l], 'w_o': w_o[l], 'g_ffn': g_ffn[l],
             'w_up': w_up[l], 'w_conv': w_conv[l], 'b_conv': b_conv[l], 'w_down': w_down[l],
             'g_ple': g_ple[l], 'w_ple_gate': w_ple_gate[l], 'w_ple_proj': w_ple_proj[l]}
        att_p = functools.partial(attend_prompt, pos=pos_p)
        att_s = functools.partial(attend_paged, pos=pos_s, pool_ckv=cache_ckv[l],
                                  pool_kr=cache_krope[l], page_table=page_table)
        xp, *new_p = hybrid_layer(xp, p_prompt[l], pos_p, att_p, C0p, n0p, m0p, conv0p, w)
        xs, *new_s = hybrid_layer(xs, p_sample[l], pos_s, att_s, state_C[l], state_n[l],
                                  state_m[l], state_conv[l], w)
        for j in range(6):
            outs_p[j].append(new_p[j])
            outs_s[j].append(new_s[j])
    ckv_p, kr_p, C_p, n_p, m_p, conv_p = [jnp.stack(o) for o in outs_p]
    ckv_s, kr_s, C_s, n_s, m_s, conv_s = [jnp.stack(o) for o in outs_s]
    y_prompt = rmsnorm(xp, g_final)
    y_sample = rmsnorm(xs, g_final)
    return (y_prompt, y_sample, ckv_p, kr_p, C_p, n_p, m_p, conv_p, ckv_s, kr_s, C_s, n_s, m_s, conv_s)
```

```python
import functools

import jax
import jax.numpy as jnp
import numpy as np
from jax import lax
from jax.experimental import pallas as pl
from jax.experimental.pallas import tpu as pltpu

D_MODEL = 4096
DEPTH = 1
PAGE_SIZE = 128
A_HEADS = 16
A_NOPE = 128
A_ROPE = 64
A_VDIM = 128
Q_LORA = 1024
KV_LORA = 512
ROPE_THETA = 10000.0
A_SCALE = (A_NOPE + A_ROPE) ** -0.5
M_HEADS = 4
M_DQK = 256
M_DV = 512
D_FF = 11008
CONV_W = 3
PLE_DIM = 256
EPS = 1e-6

LANES = 128
VMEM_LIMIT = 56 * 1024 * 1024
ROW_TILE = 1024
COL_TILE = 512
MLSTM_CHUNK = 256
MLSTM_MIN_ROWS = 128
ATT_BLOCK = 256
PAGES_PER_STEP = 8

F32 = jnp.float32
BF16 = jnp.bfloat16


def _params(n_grid):
    return pltpu.CompilerParams(dimension_semantics=("arbitrary",) * n_grid,
                                vmem_limit_bytes=VMEM_LIMIT)


def _tile(n, pref, unit=LANES):
    if n <= pref:
        return n
    best = None
    for t in range(unit, pref + 1, unit):
        if n % t == 0:
            best = t
    assert best is not None, (n, pref)
    return best


def _padded_layout():
    hq, hv = M_HEADS * M_DQK, M_HEADS * M_DV
    widths = [("cq", Q_LORA), ("ckv", KV_LORA), ("kr2", 2 * A_ROPE), ("mif", LANES),
              ("mq", hq), ("mk", hq)]
    off, lay = 0, {}
    for name, w in widths:
        lay[name] = (off, w)
        off += w
    off = -(-off // M_DV) * M_DV
    for name, w in [("mv", hv), ("mo", hv), ("ga", D_MODEL), ("gb", D_MODEL)]:
        lay[name] = (off, w)
        off += w
    return lay, off


def _dot(a, b):
    return jnp.dot(a, b, preferred_element_type=F32)


def _dot_nt(a, b):
    return lax.dot_general(a, b, (((1,), (1,)), ((), ())), preferred_element_type=F32)


def _dot_tn(a, b):
    return lax.dot_general(a, b, (((0,), (0,)), ((), ())), preferred_element_type=F32)


def _rmsnorm_kernel(x_ref, g_ref, o_ref):
    x = x_ref[...]
    y = x * lax.rsqrt(jnp.mean(x * x, axis=-1, keepdims=True) + EPS)
    o_ref[...] = (y * g_ref[...]).astype(o_ref.dtype)


def rmsnorm_rows(x, g, out_dtype):
    m, d = x.shape
    tm = _tile(m, 512, 8)
    return pl.pallas_call(
        _rmsnorm_kernel,
        grid=(m // tm,),
        in_specs=[pl.BlockSpec((tm, d), lambda i: (i, 0)),
                  pl.BlockSpec((1, d), lambda i: (0, 0))],
        out_specs=pl.BlockSpec((tm, d), lambda i: (i, 0)),
        out_shape=jax.ShapeDtypeStruct((m, d), out_dtype),
        compiler_params=_params(1),
        name="rmsnorm",
    )(x, g.reshape(1, d).astype(F32))


def _matmul_kernel(*refs, n_pairs, n_extra, epilogue, nk):
    pairs = [(refs[2 * p], refs[2 * p + 1]) for p in range(n_pairs)]
    extras = refs[2 * n_pairs:2 * n_pairs + n_extra]
    o_ref = refs[2 * n_pairs + n_extra]
    if nk == 1:
        accs = [_dot(x[...], w[...]) for x, w in pairs]
        o_ref[...] = epilogue(accs, [e[...] for e in extras]).astype(o_ref.dtype)
        return
    acc_ref = refs[2 * n_pairs + n_extra + 1]
    k = pl.program_id(2)
    x, w = pairs[0]

    @pl.when(k == 0)
    def _():
        acc_ref[...] = jnp.zeros_like(acc_ref)

    acc_ref[...] += _dot(x[...], w[...])

    @pl.when(k == nk - 1)
    def _():
        o_ref[...] = epilogue([acc_ref[...]], [e[...] for e in extras]).astype(o_ref.dtype)


def fused_matmul(pairs, extras, epilogue, n_out, out_dtype, *, tm=None, tn=None, tk=None, name="matmul"):
    m = pairs[0][0].shape[0]
    tm = tm or _tile(m, ROW_TILE, 8)
    tn = tn or _tile(n_out, COL_TILE)
    kdim = pairs[0][0].shape[1]
    nk = 1 if tk is None else kdim // tk
    assert nk == 1 or len(pairs) == 1
    in_specs, args = [], []
    for x, w in pairs:
        kd = x.shape[1]
        if nk == 1:
            in_specs += [pl.BlockSpec((tm, kd), lambda i, j: (i, 0)),
                         pl.BlockSpec((kd, tn), lambda i, j: (0, j))]
        else:
            in_specs += [pl.BlockSpec((tm, tk), lambda i, j, k: (i, k)),
                         pl.BlockSpec((tk, tn), lambda i, j, k: (k, j))]
        args += [x, w]
    for arr, off in extras:
        assert off % tn == 0
        ob = off // tn
        if nk == 1:
            in_specs.append(pl.BlockSpec((tm, tn), lambda i, j, ob=ob: (i, j + ob)))
        else:
            in_specs.append(pl.BlockSpec((tm, tn), lambda i, j, k, ob=ob: (i, j + ob)))
        args.append(arr)
    if nk == 1:
        grid = (m // tm, n_out // tn)
        out_spec = pl.BlockSpec((tm, tn), lambda i, j: (i, j))
        scratch = []
    else:
        grid = (m // tm, n_out // tn, nk)
        out_spec = pl.BlockSpec((tm, tn), lambda i, j, k: (i, j))
        scratch = [pltpu.VMEM((tm, tn), F32)]
    return pl.pallas_call(
        functools.partial(_matmul_kernel, n_pairs=len(pairs), n_extra=len(extras), epilogue=epilogue, nk=nk),
        grid=grid,
        in_specs=in_specs,
        out_specs=out_spec,
        out_shape=jax.ShapeDtypeStruct((m, n_out), out_dtype),
        scratch_shapes=scratch,
        compiler_params=_params(len(grid)),
        name=name,
    )(*args)


def _ep_plain(accs, extras):
    return accs[0]


def _ep_residual(accs, extras):
    return extras[0] + accs[0]


def _ep_gated_merge(accs, extras):
    return jax.nn.sigmoid(extras[0]) * accs[0] + jax.nn.sigmoid(extras[1]) * accs[1]


def _ep_ple(accs, extras):
    return extras[0] + jax.nn.sigmoid(accs[0]) * accs[1]


def _rope_mix(x, cos, sin):
    return x * cos + pltpu.roll(x, A_ROPE // 2, 1) * sin


def _mla_prep_kernel(cq_ref, ckv_ref, kr_ref, gq_ref, gkv_ref, cos_ref, sin_ref,
                     cqn_ref, ckv_out_ref, ckvb_ref, krope_ref):
    cq = cq_ref[...]
    cqn = cq * lax.rsqrt(jnp.mean(cq * cq, axis=-1, keepdims=True) + EPS) * gq_ref[...]
    cqn_ref[...] = cqn.astype(cqn_ref.dtype)
    ckv = ckv_ref[...]
    ckvn = ckv * lax.rsqrt(jnp.mean(ckv * ckv, axis=-1, keepdims=True) + EPS) * gkv_ref[...]
    ckv_out_ref[...] = ckvn
    ckvb_ref[...] = ckvn.astype(ckvb_ref.dtype)
    krope_ref[...] = _rope_mix(kr_ref[...], cos_ref[...], sin_ref[...])[:, :A_ROPE]


def mla_prep(proj, lay, g_qa, g_kva, cos, sin):
    m = proj.shape[0]
    tm = _tile(m, 512, 8)
    (o_cq, w_cq), (o_ckv, w_ckv), (o_kr, w_kr) = lay["cq"], lay["ckv"], lay["kr2"]
    assert o_cq % w_cq == 0 and o_ckv % w_ckv == 0 and o_kr % w_kr == 0
    return pl.pallas_call(
        _mla_prep_kernel,
        grid=(m // tm,),
        in_specs=[pl.BlockSpec((tm, w_cq), lambda i: (i, o_cq // w_cq)),
                  pl.BlockSpec((tm, w_ckv), lambda i: (i, o_ckv // w_ckv)),
                  pl.BlockSpec((tm, w_kr), lambda i: (i, o_kr // w_kr)),
                  pl.BlockSpec((1, w_cq), lambda i: (0, 0)),
                  pl.BlockSpec((1, w_ckv), lambda i: (0, 0)),
                  pl.BlockSpec((tm, LANES), lambda i: (i, 0)),
                  pl.BlockSpec((tm, LANES), lambda i: (i, 0))],
        out_specs=[pl.BlockSpec((tm, w_cq), lambda i: (i, 0)),
                   pl.BlockSpec((tm, w_ckv), lambda i: (i, 0)),
                   pl.BlockSpec((tm, w_ckv), lambda i: (i, 0)),
                   pl.BlockSpec((tm, A_ROPE), lambda i: (i, 0))],
        out_shape=[jax.ShapeDtypeStruct((m, w_cq), BF16),
                   jax.ShapeDtypeStruct((m, w_ckv), F32),
                   jax.ShapeDtypeStruct((m, w_ckv), BF16),
                   jax.ShapeDtypeStruct((m, A_ROPE), F32)],
        compiler_params=_params(1),
        name="mla_prep",
    )(proj, proj, proj, g_qa.reshape(1, -1), g_kva.reshape(1, -1), cos, sin)


def _q_prep_kernel(x_ref, w_ref, cos_ref, sin_ref, *rest, absorbed):
    acc = _dot(x_ref[...], w_ref[...])
    nope = (acc[:, :A_NOPE] * A_SCALE).astype(BF16)
    rot = _rope_mix(acc[:, A_NOPE:], cos_ref[...], sin_ref[...])[:, :A_ROPE] * A_SCALE
    if absorbed:
        wuk_ref, o_ref = rest
        o_ref[:, :KV_LORA] = _dot(nope, wuk_ref[...]).astype(o_ref.dtype)
        o_ref[:, KV_LORA:] = rot.astype(o_ref.dtype)
    else:
        (o_ref,) = rest
        o_ref[:, :A_NOPE] = nope
        o_ref[:, A_NOPE:] = rot.astype(o_ref.dtype)


def q_prep(cqn, wq, cos, sin, wuk_t=None):
    m, kq = cqn.shape
    tm = _tile(m, ROW_TILE, 8)
    wcols = wq.shape[2]
    absorbed = wuk_t is not None
    width = (KV_LORA if absorbed else A_NOPE) + A_ROPE
    in_specs = [pl.BlockSpec((tm, kq), lambda h, i: (i, 0)),
                pl.BlockSpec((None, kq, wcols), lambda h, i: (h, 0, 0)),
                pl.BlockSpec((tm, LANES), lambda h, i: (i, 0)),
                pl.BlockSpec((tm, LANES), lambda h, i: (i, 0))]
    args = [cqn, wq, cos, sin]
    if absorbed:
        in_specs.append(pl.BlockSpec((None, A_NOPE, KV_LORA), lambda h, i: (h, 0, 0)))
        args.append(wuk_t)
    return pl.pallas_call(
        functools.partial(_q_prep_kernel, absorbed=absorbed),
        grid=(A_HEADS, m // tm),
        in_specs=in_specs,
        out_specs=pl.BlockSpec((None, tm, width), lambda h, i: (h, i, 0)),
        out_shape=jax.ShapeDtypeStruct((A_HEADS, m, width), BF16),
        compiler_params=_params(2),
        name="q_prep",
    )(*args)


def _kv_prep_kernel(x_ref, wk_ref, wv_ref, kr_ref, k_ref, v_ref):
    x = x_ref[...]
    k_ref[:, :A_NOPE] = _dot(x, wk_ref[...]).astype(k_ref.dtype)
    k_ref[:, A_NOPE:] = kr_ref[...].astype(k_ref.dtype)
    v_ref[...] = _dot(x, wv_ref[...]).astype(v_ref.dtype)


def kv_prep(ckv_b, krope, wk, wv):
    m = ckv_b.shape[0]
    tm = _tile(m, ROW_TILE, 8)
    return pl.pallas_call(
        _kv_prep_kernel,
        grid=(A_HEADS, m // tm),
        in_specs=[pl.BlockSpec((tm, KV_LORA), lambda h, i: (i, 0)),
                  pl.BlockSpec((None, KV_LORA, A_NOPE), lambda h, i: (h, 0, 0)),
                  pl.BlockSpec((None, KV_LORA, A_VDIM), lambda h, i: (h, 0, 0)),
                  pl.BlockSpec((tm, A_ROPE), lambda h, i: (i, 0))],
        out_specs=[pl.BlockSpec((None, tm, A_NOPE + A_ROPE), lambda h, i: (h, i, 0)),
                   pl.BlockSpec((None, tm, A_VDIM), lambda h, i: (h, i, 0))],
        out_shape=[jax.ShapeDtypeStruct((A_HEADS, m, A_NOPE + A_ROPE), BF16),
                   jax.ShapeDtypeStruct((A_HEADS, m, A_VDIM), BF16)],
        compiler_params=_params(2),
        name="kv_prep",
    )(ckv_b, wk, wv, krope)


def _softmax_step(s, v, m, l, acc):
    m_new = jnp.maximum(m, jnp.max(s, axis=1, keepdims=True))
    p = jnp.exp(s - m_new)
    alpha = jnp.exp(m - m_new)
    l = alpha * l + jnp.sum(p, axis=1, keepdims=True)
    acc = alpha * acc + _dot(p.astype(BF16), v)
    return m_new, l, acc


def _flash_kernel(q_ref, k_ref, v_ref, o_ref, *, blk):
    qi = pl.program_id(2)
    q = q_ref[...]

    def body(kj, carry):
        start = pl.multiple_of(kj * blk, blk)
        s = _dot_nt(q, k_ref[pl.ds(start, blk), :])
        return _softmax_step(s, v_ref[pl.ds(start, blk), :], *carry)

    init = (jnp.full((blk, 1), -jnp.inf, F32), jnp.zeros((blk, 1), F32), jnp.zeros((blk, A_VDIM), F32))
    carry = lax.fori_loop(0, qi, body, init)
    start = pl.multiple_of(qi * blk, blk)
    s = _dot_nt(q, k_ref[pl.ds(start, blk), :])
    row = lax.broadcasted_iota(jnp.int32, (blk, blk), 0)
    col = lax.broadcasted_iota(jnp.int32, (blk, blk), 1)
    s = jnp.where(col <= row, s, -jnp.inf)
    _, l, acc = _softmax_step(s, v_ref[pl.ds(start, blk), :], *carry)
    o_ref[...] = (acc / l).astype(o_ref.dtype)


def flash_prompt(q, k, v, batch, seq):
    blk = _tile(seq, ATT_BLOCK, 8)
    nq = seq // blk
    dqk = q.shape[2]
    return pl.pallas_call(
        functools.partial(_flash_kernel, blk=blk),
        grid=(batch, A_HEADS, nq),
        in_specs=[pl.BlockSpec((None, blk, dqk), lambda b, h, i: (h, b * nq + i, 0)),
                  pl.BlockSpec((None, seq, dqk), lambda b, h, i: (h, b, 0)),
                  pl.BlockSpec((None, seq, A_VDIM), lambda b, h, i: (h, b, 0))],
        out_specs=pl.BlockSpec((blk, A_VDIM), lambda b, h, i: (b * nq + i, h)),
        out_shape=jax.ShapeDtypeStruct((batch * seq, A_HEADS * A_VDIM), BF16),
        compiler_params=_params(3),
        name="flash_prompt",
    )(q, k, v)


def _paged_kernel(pt_ref, q_ref, *refs, n_steps, dec_seq):
    pp = PAGES_PER_STEP
    ckv_refs, kr_refs = refs[:pp], refs[pp:2 * pp]
    ckv_new_ref, kr_new_ref, o_ref, m_ref, l_ref, acc_ref = refs[2 * pp:]
    g = pl.program_id(1)

    @pl.when(g == 0)
    def _():
        m_ref[...] = jnp.full_like(m_ref, -jnp.inf)
        l_ref[...] = jnp.zeros_like(l_ref)
        acc_ref[...] = jnp.zeros_like(acc_ref)

    q = q_ref[...]
    q_lat, q_rope = q[:, :KV_LORA], q[:, KV_LORA:]
    kc = jnp.concatenate([r[...].astype(BF16) for r in ckv_refs], axis=0)
    kr = jnp.concatenate([r[...].astype(BF16) for r in kr_refs], axis=0)
    s = _dot_nt(q_lat, kc) + _dot_nt(q_rope, kr)
    m, l, acc = _softmax_step(s, kc, m_ref[...], l_ref[...], acc_ref[...])
    m_ref[...] = m
    l_ref[...] = l
    acc_ref[...] = acc

    @pl.when(g == n_steps - 1)
    def _():
        def pad_rows(x):
            return jnp.concatenate([x, jnp.zeros((PAGE_SIZE - dec_seq, x.shape[1]), x.dtype)], axis=0)

        kcn = pad_rows(ckv_new_ref[...]).astype(BF16)
        krn = pad_rows(kr_new_ref[...]).astype(BF16)
        sn = _dot_nt(q_lat, kcn) + _dot_nt(q_rope, krn)
        tok = lax.broadcasted_iota(jnp.int32, sn.shape, 0) % dec_seq
        key = lax.broadcasted_iota(jnp.int32, sn.shape, 1)
        sn = jnp.where(key <= tok, sn, -jnp.inf)
        _, l2, acc2 = _softmax_step(sn, kcn, m_ref[...], l_ref[...], acc_ref[...])
        o_ref[...] = (acc2 / l2).astype(o_ref.dtype)


def paged_attention(q, pool_ckv, pool_kr, page_table, ckv_new, kr_new):
    nb, rows, dq = q.shape
    n_pages = page_table.shape[1]
    dec_seq = ckv_new.shape[1]
    pp = PAGES_PER_STEP
    assert n_pages % pp == 0
    n_steps = n_pages // pp
    page = pool_ckv.shape[1]

    def page_map(i):
        return lambda b, g, pt: (pt[b * n_pages + g * pp + i], 0, 0)

    in_specs = [pl.BlockSpec((None, rows, dq), lambda b, g, pt: (b, 0, 0))]
    in_specs += [pl.BlockSpec((None, page, KV_LORA), page_map(i)) for i in range(pp)]
    in_specs += [pl.BlockSpec((None, page, A_ROPE), page_map(i)) for i in range(pp)]
    in_specs += [pl.BlockSpec((None, dec_seq, KV_LORA), lambda b, g, pt: (b, 0, 0)),
                 pl.BlockSpec((None, dec_seq, A_ROPE), lambda b, g, pt: (b, 0, 0))]
    grid_spec = pltpu.PrefetchScalarGridSpec(
        num_scalar_prefetch=1,
        grid=(nb, n_steps),
        in_specs=in_specs,
        out_specs=pl.BlockSpec((None, rows, KV_LORA), lambda b, g, pt: (b, 0, 0)),
        scratch_shapes=[pltpu.VMEM((rows, 1), F32), pltpu.VMEM((rows, 1), F32),
                        pltpu.VMEM((rows, KV_LORA), F32)],
    )
    return pl.pallas_call(
        functools.partial(_paged_kernel, n_steps=n_steps, dec_seq=dec_seq),
        grid_spec=grid_spec,
        out_shape=jax.ShapeDtypeStruct((nb, rows, KV_LORA), BF16),
        compiler_params=_params(2),
        name="paged_attention",
    )(page_table.reshape(-1), q, *([pool_ckv] * pp), *([pool_kr] * pp), ckv_new, kr_new)


def _head_mm_kernel(x_ref, w_ref, o_ref):
    o_ref[...] = _dot(x_ref[...], w_ref[...]).astype(o_ref.dtype)


def head_matmul(x, w):
    nh, m, kd = x.shape
    n = w.shape[2]
    return pl.pallas_call(
        _head_mm_kernel,
        grid=(nh,),
        in_specs=[pl.BlockSpec((None, m, kd), lambda h: (h, 0, 0)),
                  pl.BlockSpec((None, kd, n), lambda h: (h, 0, 0))],
        out_specs=pl.BlockSpec((m, n), lambda h: (0, h)),
        out_shape=jax.ShapeDtypeStruct((m, nh * n), BF16),
        compiler_params=_params(1),
        name="head_matmul",
    )(x, w)


def _mlstm_kernel(q_ref, k_ref, v_ref, if_ref, mo_ref, bias_ref, gmh_ref, c0_ref, n0_ref, m0_ref,
                  o_ref, c_ref, n_ref, m_ref, *, rows, lp):
    head = pl.program_id(1)
    chunk = pl.program_id(2)

    @pl.when(chunk == 0)
    def _():
        c_ref[...] = c0_ref[...]
        n_ref[...] = n0_ref[...]
        m_ref[...] = m0_ref[...]

    def pad(x):
        if rows == lp:
            return x
        return jnp.concatenate([x, jnp.zeros((lp - rows, x.shape[1]), x.dtype)], axis=0)

    q = pad(q_ref[...]) * (M_DQK ** -0.5)
    k = pad(k_ref[...])
    v = pad(v_ref[...])
    gates = pad(if_ref[...] + bias_ref[...])
    lane = lax.broadcasted_iota(jnp.int32, gates.shape, 1)
    i_col = jnp.sum(jnp.where(lane == head, gates, 0.0), axis=1, keepdims=True)
    f_pre = jnp.sum(jnp.where(lane == head + M_HEADS, gates, 0.0), axis=1, keepdims=True)
    f_col = jnp.minimum(f_pre, 0.0) - jnp.log1p(jnp.exp(-jnp.abs(f_pre)))
    if rows != lp:
        valid = lax.broadcasted_iota(jnp.int32, (lp, 1), 0) < rows
        i_col = jnp.where(valid, i_col, -jnp.inf)
        f_col = jnp.where(valid, f_col, 0.0)

    t_idx = lax.broadcasted_iota(jnp.int32, (lp, lp), 0)
    s_idx = lax.broadcasted_iota(jnp.int32, (lp, lp), 1)
    causal = s_idx <= t_idx
    diag = s_idx == t_idx
    f_row = jnp.sum(jnp.where(diag, f_col, 0.0), axis=0, keepdims=True)
    i_row = jnp.sum(jnp.where(diag, i_col, 0.0), axis=0, keepdims=True)
    b_col = jnp.sum(jnp.where(causal, f_row, 0.0), axis=1, keepdims=True)
    b_row = jnp.sum(jnp.where(t_idx <= s_idx, f_col, 0.0), axis=0, keepdims=True)

    m_prev = m_ref[...]
    c_prev = c_ref[...]
    n_prev = n_ref[...]
    a_col = b_col + m_prev
    dmat = jnp.where(causal, b_col - b_row + i_row, -jnp.inf)
    mt = jnp.maximum(a_col, jnp.max(dmat, axis=1, keepdims=True))
    w_inter = jnp.exp(a_col - mt)
    qb, kb, vb = q.astype(BF16), k.astype(BF16), v.astype(BF16)
    qk = _dot_nt(qb, kb) * jnp.exp(dmat - mt)
    num = _dot(qk.astype(BF16), vb) + w_inter * _dot(qb, c_prev.astype(BF16))
    den = jnp.sum(qk, axis=1, keepdims=True) + w_inter * jnp.sum(q * n_prev, axis=1, keepdims=True)
    hid = num / jnp.maximum(jnp.abs(den), jnp.exp(-mt))

    b_last = b_col[lp - 1:lp, :]
    g_col = b_last - b_col + i_col
    m_new = jnp.maximum(b_last + m_prev, jnp.max(g_col, axis=0, keepdims=True))
    decay = jnp.exp(b_last + m_prev - m_new)
    kw = jnp.exp(g_col - m_new) * k
    c_ref[...] = decay * c_prev + _dot_tn(kw.astype(BF16), vb)
    n_ref[...] = decay * n_prev + jnp.sum(kw, axis=0, keepdims=True)
    m_ref[...] = m_new

    hn = hid * lax.rsqrt(jnp.mean(hid * hid, axis=1, keepdims=True) + EPS) * gmh_ref[...]
    out = jax.nn.sigmoid(pad(mo_ref[...])) * hn
    o_ref[...] = out[:rows].astype(o_ref.dtype)


def mlstm(proj, lay, bias_row, g_mh, c0, n0, m0, batch, seq):
    rows = _tile(seq, MLSTM_CHUNK, 8)
    lp = max(rows, MLSTM_MIN_ROWS)
    nc = seq // rows
    bh = batch * M_HEADS
    oq, ok, ov, oi, oo = (lay[n][0] for n in ("mq", "mk", "mv", "mif", "mo"))
    assert oq % M_DQK == 0 and ok % M_DQK == 0 and ov % M_DV == 0 and oo % M_DV == 0 and oi % LANES == 0

    def rowmap(col_of_head):
        return lambda b, h, c: (b * nc + c, col_of_head(h))

    def state_map(b, h, c):
        return (b * M_HEADS + h, 0, 0)

    out, c1, n1, m1 = pl.pallas_call(
        functools.partial(_mlstm_kernel, rows=rows, lp=lp),
        grid=(batch, M_HEADS, nc),
        in_specs=[pl.BlockSpec((rows, M_DQK), rowmap(lambda h: oq // M_DQK + h)),
                  pl.BlockSpec((rows, M_DQK), rowmap(lambda h: ok // M_DQK + h)),
                  pl.BlockSpec((rows, M_DV), rowmap(lambda h: ov // M_DV + h)),
                  pl.BlockSpec((rows, LANES), rowmap(lambda h: oi // LANES)),
                  pl.BlockSpec((rows, M_DV), rowmap(lambda h: oo // M_DV + h)),
                  pl.BlockSpec((1, LANES), lambda b, h, c: (0, 0)),
                  pl.BlockSpec((None, 1, M_DV), lambda b, h, c: (h, 0, 0)),
                  pl.BlockSpec((None, M_DQK, M_DV), state_map),
                  pl.BlockSpec((None, 1, M_DQK), state_map),
                  pl.BlockSpec((None, 1, 1), state_map)],
        out_specs=[pl.BlockSpec((rows, M_DV), lambda b, h, c: (b * nc + c, h)),
                   pl.BlockSpec((None, M_DQK, M_DV), state_map),
                   pl.BlockSpec((None, 1, M_DQK), state_map),
                   pl.BlockSpec((None, 1, 1), state_map)],
        out_shape=[jax.ShapeDtypeStruct((batch * seq, M_HEADS * M_DV), BF16),
                   jax.ShapeDtypeStruct((bh, M_DQK, M_DV), F32),
                   jax.ShapeDtypeStruct((bh, 1, M_DQK), F32),
                   jax.ShapeDtypeStruct((bh, 1, 1), F32)],
        compiler_params=_params(3),
        name="mlstm",
    )(proj, proj, proj, proj, proj, bias_row, g_mh.reshape(M_HEADS, 1, M_DV),
      c0.reshape(bh, M_DQK, M_DV), n0.reshape(bh, 1, M_DQK), m0.reshape(bh, 1, 1))
    return (out, c1.reshape(batch, M_HEADS, M_DQK, M_DV), n1.reshape(batch, M_HEADS, M_DQK),
            m1.reshape(batch, M_HEADS))


def _convglu_kernel(g_ref, v_ref, c0_ref, w_ref, b_ref, act_ref, tail_ref):
    g = g_ref[...]
    seq = g.shape[1]
    c0 = c0_ref[...]
    t = lax.broadcasted_iota(jnp.int32, g.shape, 1)
    prev1 = jnp.where(t == 0, c0[:, 1:2, :], pltpu.roll(g, 1, 1))
    prev2 = jnp.where(t == 0, c0[:, 0:1, :], jnp.where(t == 1, c0[:, 1:2, :], pltpu.roll(g, 2, 1)))
    w = w_ref[...]
    conv = b_ref[...] + w[0:1, :] * prev2 + w[1:2, :] * prev1 + w[2:3, :] * g
    act_ref[...] = (jax.nn.gelu(conv, approximate=True) * v_ref[...]).astype(act_ref.dtype)
    tail_ref[...] = g[:, seq - (CONV_W - 1):, :]


def convglu(u, conv0, w_conv, b_conv, batch, seq):
    assert CONV_W == 3 and seq >= CONV_W - 1
    u3 = u.reshape(batch, seq, 2 * D_FF)
    tc = _tile(D_FF, 256)
    nb = max(1, min(batch, 1024 // seq))
    assert batch % nb == 0
    ncol = D_FF // tc
    act, tail = pl.pallas_call(
        _convglu_kernel,
        grid=(batch // nb, ncol),
        in_specs=[pl.BlockSpec((nb, seq, tc), lambda b, j: (b, 0, j)),
                  pl.BlockSpec((nb, seq, tc), lambda b, j: (b, 0, j + ncol)),
                  pl.BlockSpec((nb, CONV_W - 1, tc), lambda b, j: (b, 0, j)),
                  pl.BlockSpec((CONV_W, tc), lambda b, j: (0, j)),
                  pl.BlockSpec((1, tc), lambda b, j: (0, j))],
        out_specs=[pl.BlockSpec((nb, seq, tc), lambda b, j: (b, 0, j)),
                   pl.BlockSpec((nb, CONV_W - 1, tc), lambda b, j: (b, 0, j))],
        out_shape=[jax.ShapeDtypeStruct((batch, seq, D_FF), BF16),
                   jax.ShapeDtypeStruct((batch, CONV_W - 1, D_FF), F32)],
        compiler_params=_params(2),
        name="convglu",
    )(u3, u3, conv0, w_conv, b_conv.reshape(1, D_FF))
    return act.reshape(batch * seq, D_FF), tail


def _prepare_weights(w):
    lay, total = _padded_layout()
    offs = np.concatenate([[0], np.cumsum([Q_LORA, KV_LORA, A_ROPE, M_HEADS * M_DQK, M_HEADS * M_DQK,
                                            M_HEADS * M_DV, M_HEADS, M_HEADS, M_HEADS * M_DV, D_MODEL, D_MODEL])])
    src = {n: (int(offs[i]), int(offs[i + 1])) for i, n in enumerate(
        ["cq", "ckv", "kr", "mq", "mk", "mv", "mi", "mf", "mo", "ga", "gb"])}
    w_in = w["w_in"]

    def cols(name):
        return w_in[:, src[name][0]:src[name][1]]

    pieces, off = [], 0
    def put(name, block):
        nonlocal off
        o, wd = lay[name]
        if o > off:
            pieces.append(jnp.zeros((D_MODEL, o - off), w_in.dtype))
        pieces.append(block)
        if block.shape[1] < wd:
            pieces.append(jnp.zeros((D_MODEL, wd - block.shape[1]), w_in.dtype))
        off = o + wd

    put("cq", cols("cq"))
    put("ckv", cols("ckv"))
    put("kr2", jnp.concatenate([cols("kr"), cols("kr")], axis=1))
    put("mif", jnp.concatenate([cols("mi"), cols("mf")], axis=1))
    for n in ("mq", "mk", "mv", "mo", "ga", "gb"):
        put(n, cols(n))
    assert off == total
    w_in_p = jnp.concatenate(pieces, axis=1).astype(BF16)

    w_uq = w["w_uq"]
    rope_cols = w_uq[..., A_NOPE:]
    wq = jnp.concatenate([w_uq[..., :A_NOPE], rope_cols, rope_cols], axis=-1).transpose(1, 0, 2).astype(BF16)
    bias_row = jnp.concatenate([w["b_i"].astype(F32), w["b_f"].astype(F32),
                                jnp.zeros((LANES - 2 * M_HEADS,), F32)]).reshape(1, LANES)
    return {
        "lay": lay, "n_proj": total, "w_in_p": w_in_p, "wq": wq,
        "wuk_t": w["w_uk"].transpose(1, 2, 0).astype(BF16),
        "wk": w["w_uk"].transpose(1, 0, 2).astype(BF16),
        "wv": w["w_uv"].transpose(1, 0, 2).astype(BF16),
        "bias_row": bias_row,
        "w_br_a": w["w_br_a"].astype(BF16), "w_br_b": w["w_br_b"].astype(BF16),
        "w_o": w["w_o"].astype(BF16), "w_up": w["w_up"].astype(BF16), "w_down": w["w_down"].astype(BF16),
        "w_ple_gate": w["w_ple_gate"].astype(BF16), "w_ple_proj": w["w_ple_proj"].astype(BF16),
    }


def _rope_tables(pos, batch):
    half = A_ROPE // 2
    inv = jnp.power(ROPE_THETA, -jnp.arange(half, dtype=F32) / half)
    ang = pos[:, None] * inv[None, :]
    cos, sin = jnp.cos(ang), jnp.sin(ang)
    reps = LANES // A_ROPE
    cos_t = jnp.tile(jnp.concatenate([cos, cos], axis=1), (batch, reps))
    sin_t = jnp.tile(jnp.concatenate([-sin, sin], axis=1), (batch, reps))
    return cos_t, sin_t


def _hybrid_layer(x, pe, pos, w, pw, c0, n0, m0, conv0, paged):
    batch, seq, _ = x.shape
    m = batch * seq
    lay = pw["lay"]
    x2 = x.reshape(m, D_MODEL)
    cos, sin = _rope_tables(pos, batch)

    h = rmsnorm_rows(x2, w["g_mix"], BF16)
    proj = fused_matmul([(h, pw["w_in_p"])], [], _ep_plain, pw["n_proj"], F32,
                        tn=_tile(pw["n_proj"], 1024), name="in_proj")
    cqn, c_kv, ckv_b, k_rope = mla_prep(proj, lay, w["g_qa"], w["g_kva"], cos, sin)

    if paged is None:
        q = q_prep(cqn, pw["wq"], cos, sin)
        k, v = kv_prep(ckv_b, k_rope, pw["wk"], pw["wv"])
        a_out = flash_prompt(q, k, v, batch, seq)
    else:
        pool_ckv, pool_kr, page_table = paged
        q = q_prep(cqn, pw["wq"], cos, sin, pw["wuk_t"])
        dq = q.shape[2]
        q = q.reshape(A_HEADS, batch, seq, dq).transpose(1, 0, 2, 3).reshape(batch, A_HEADS * seq, dq)
        o_lat = paged_attention(q, pool_ckv, pool_kr, page_table,
                                c_kv.reshape(batch, seq, KV_LORA), k_rope.reshape(batch, seq, A_ROPE))
        o_lat = o_lat.reshape(batch, A_HEADS, seq, KV_LORA).transpose(1, 0, 2, 3).reshape(A_HEADS, m, KV_LORA)
        a_out = head_matmul(o_lat, pw["wv"])

    b_out, c1, n1, m1 = mlstm(proj, lay, pw["bias_row"], w["g_mh"], c0, n0, m0, batch, seq)

    merged = fused_matmul([(a_out, pw["w_br_a"]), (b_out, pw["w_br_b"])],
                          [(proj, lay["ga"][0]), (proj, lay["gb"][0])],
                          _ep_gated_merge, D_MODEL, BF16, name="branch_merge")
    x2 = fused_matmul([(merged, pw["w_o"])], [(x2, 0)], _ep_residual, D_MODEL, F32, name="out_proj")

    hf = rmsnorm_rows(x2, w["g_ffn"], BF16)
    u = fused_matmul([(hf, pw["w_up"])], [], _ep_plain, 2 * D_FF, F32, tn=_tile(2 * D_FF, 512), name="ffn_up")
    act, conv_tail = convglu(u, conv0.astype(F32), w["w_conv"], w["b_conv"], batch, seq)
    x2 = fused_matmul([(act, pw["w_down"])], [(x2, 0)], _ep_residual, D_MODEL, F32,
                      tn=_tile(D_MODEL, 512), tk=_tile(D_FF, D_FF // 2), name="ffn_down")

    hp = rmsnorm_rows(x2, w["g_ple"], BF16)
    x2 = fused_matmul([(hp, pw["w_ple_gate"]), (pe.reshape(m, PLE_DIM).astype(BF16), pw["w_ple_proj"])],
                      [(x2, 0)], _ep_ple, D_MODEL, F32, name="ple")
    return (x2, c_kv.reshape(batch, seq, KV_LORA), k_rope.reshape(batch, seq, A_ROPE), c1, n1, m1, conv_tail)


def kernel(x_prompt, x_sample, cache_ckv, cache_krope, state_C, state_n, state_m, state_conv, page_table,
           p_prompt, p_sample, g_mix, w_in, b_i, b_f, g_qa, w_uq, g_kva, w_uk, w_uv, g_mh, w_br_a, w_br_b,
           w_o, g_ffn, w_up, w_conv, b_conv, w_down, g_ple, w_ple_gate, w_ple_proj, g_final):
    bp, sp = x_prompt.shape[0], x_prompt.shape[1]
    bs, ss = x_sample.shape[0], x_sample.shape[1]
    depth = w_in.shape[0]
    past = page_table.shape[1] * cache_ckv.shape[2]
    pos_p = jnp.arange(sp, dtype=F32)
    pos_s = past + jnp.arange(ss, dtype=F32)
    c0p = jnp.zeros((bp, M_HEADS, M_DQK, M_DV), F32)
    n0p = jnp.zeros((bp, M_HEADS, M_DQK), F32)
    m0p = jnp.zeros((bp, M_HEADS), F32)
    conv0p = jnp.zeros((bp, CONV_W - 1, D_FF), F32)
    names = ["g_mix", "w_in", "b_i", "b_f", "g_qa", "w_uq", "g_kva", "w_uk", "w_uv", "g_mh", "w_br_a", "w_br_b",
             "w_o", "g_ffn", "w_up", "w_conv", "b_conv", "w_down", "g_ple", "w_ple_gate", "w_ple_proj"]
    stacked = [g_mix, w_in, b_i, b_f, g_qa, w_uq, g_kva, w_uk, w_uv, g_mh, w_br_a, w_br_b,
               w_o, g_ffn, w_up, w_conv, b_conv, w_down, g_ple, w_ple_gate, w_ple_proj]
    xp, xs = x_prompt, x_sample
    outs_p = [[] for _ in range(6)]
    outs_s = [[] for _ in range(6)]
    for l in range(depth):
        w = {n: a[l] for n, a in zip(names, stacked)}
        pw = _prepare_weights(w)
        xp2, *new_p = _hybrid_layer(xp, p_prompt[l], pos_p, w, pw, c0p, n0p, m0p, conv0p, None)
        xs2, *new_s = _hybrid_layer(xs, p_sample[l], pos_s, w, pw, state_C[l], state_n[l], state_m[l],
                                    state_conv[l], (cache_ckv[l], cache_krope[l], page_table))
        xp, xs = xp2.reshape(xp.shape), xs2.reshape(xs.shape)
        for j in range(6):
            outs_p[j].append(new_p[j])
            outs_s[j].append(new_s[j])
    stack_p = [jnp.stack(o) for o in outs_p]
    stack_s = [jnp.stack(o) for o in outs_s]
    y_prompt = rmsnorm_rows(xp.reshape(bp * sp, D_MODEL), g_final, F32).reshape(xp.shape)
    y_sample = rmsnorm_rows(xs.reshape(bs * ss, D_MODEL), g_final, F32).reshape(xs.shape)
    return (y_prompt, y_sample, *stack_p, *stack_s)
```

```python
import functools

import jax
import jax.numpy as jnp
import numpy as np
from jax import lax
from jax.experimental import pallas as pl
from jax.experimental.pallas import tpu as pltpu

D_MODEL = 4096
DEPTH = 1
PAGE_SIZE = 128
A_HEADS = 16
A_NOPE = 128
A_ROPE = 64
A_VDIM = 128
Q_LORA = 1024
KV_LORA = 512
ROPE_THETA = 10000.0
A_SCALE = (A_NOPE + A_ROPE) ** -0.5
M_HEADS = 4
M_DQK = 256
M_DV = 512
D_FF = 11008
CONV_W = 3
PLE_DIM = 256
EPS = 1e-6

LANES = 128
VMEM_LIMIT = 56 * 1024 * 1024
ROW_TILE = 1024
COL_TILE = 512
MLSTM_CHUNK = 256
MLSTM_MIN_ROWS = 128
ATT_BLOCK = 256
ATT_HEADS_PER_STEP = 4
PAGES_PER_STEP = 16
PAGES_PER_GROUP = 8

F32 = jnp.float32
BF16 = jnp.bfloat16


def _params(n_grid):
    return pltpu.CompilerParams(dimension_semantics=("arbitrary",) * n_grid,
                                vmem_limit_bytes=VMEM_LIMIT)


def _tile(n, pref, unit=LANES):
    if n <= pref:
        return n
    best = None
    for t in range(unit, pref + 1, unit):
        if n % t == 0:
            best = t
    assert best is not None, (n, pref)
    return best


def _padded_layout():
    hq, hv = M_HEADS * M_DQK, M_HEADS * M_DV
    widths = [("cq", Q_LORA), ("mq", hq), ("mk", hq), ("ckv", KV_LORA), ("kr2", 2 * A_ROPE), ("mif", LANES)]
    off, lay = 0, {}
    for name, w in widths:
        lay[name] = (off, w)
        off += w
    off = -(-off // hv) * hv
    for name, w in [("mv", hv), ("mo", hv), ("ga", D_MODEL), ("gb", D_MODEL)]:
        lay[name] = (off, w)
        off += w
    return lay, off


def _dot(a, b):
    return jnp.dot(a, b, preferred_element_type=F32)


def _dot_nt(a, b):
    return lax.dot_general(a, b, (((1,), (1,)), ((), ())), preferred_element_type=F32)


def _dot_tn(a, b):
    return lax.dot_general(a, b, (((0,), (0,)), ((), ())), preferred_element_type=F32)


def _rmsnorm_kernel(x_ref, g_ref, o_ref):
    x = x_ref[...]
    y = x * lax.rsqrt(jnp.mean(x * x, axis=-1, keepdims=True) + EPS)
    o_ref[...] = (y * g_ref[...]).astype(o_ref.dtype)


def rmsnorm_rows(x, g, out_dtype):
    m, d = x.shape
    tm = _tile(m, 512, 8)
    return pl.pallas_call(
        _rmsnorm_kernel,
        grid=(m // tm,),
        in_specs=[pl.BlockSpec((tm, d), lambda i: (i, 0)),
                  pl.BlockSpec((1, d), lambda i: (0, 0))],
        out_specs=pl.BlockSpec((tm, d), lambda i: (i, 0)),
        out_shape=jax.ShapeDtypeStruct((m, d), out_dtype),
        compiler_params=_params(1),
        name="rmsnorm",
    )(x, g.reshape(1, d).astype(F32))


def _matmul_kernel(*refs, n_pairs, n_extra, epilogue, nk):
    pairs = [(refs[2 * p], refs[2 * p + 1]) for p in range(n_pairs)]
    extras = refs[2 * n_pairs:2 * n_pairs + n_extra]
    o_ref = refs[2 * n_pairs + n_extra]
    if nk == 1:
        accs = [_dot(x[...], w[...]) for x, w in pairs]
        o_ref[...] = epilogue(accs, [e[...] for e in extras]).astype(o_ref.dtype)
        return
    acc_ref = refs[2 * n_pairs + n_extra + 1]
    k = pl.program_id(2)
    x, w = pairs[0]

    @pl.when(k == 0)
    def _():
        acc_ref[...] = jnp.zeros_like(acc_ref)

    acc_ref[...] += _dot(x[...], w[...])

    @pl.when(k == nk - 1)
    def _():
        o_ref[...] = epilogue([acc_ref[...]], [e[...] for e in extras]).astype(o_ref.dtype)


def fused_matmul(pairs, extras, epilogue, n_out, out_dtype, *, tm=None, tn=None, tk=None, name="matmul"):
    m = pairs[0][0].shape[0]
    tm = tm or _tile(m, ROW_TILE, 8)
    tn = tn or _tile(n_out, COL_TILE)
    kdim = pairs[0][0].shape[1]
    nk = 1 if tk is None else kdim // tk
    assert nk == 1 or len(pairs) == 1
    in_specs, args = [], []
    for x, w in pairs:
        kd = x.shape[1]
        if nk == 1:
            in_specs += [pl.BlockSpec((tm, kd), lambda i, j: (i, 0)),
                         pl.BlockSpec((kd, tn), lambda i, j: (0, j))]
        else:
            in_specs += [pl.BlockSpec((tm, tk), lambda i, j, k: (i, k)),
                         pl.BlockSpec((tk, tn), lambda i, j, k: (k, j))]
        args += [x, w]
    for arr, off in extras:
        assert off % tn == 0
        ob = off // tn
        if nk == 1:
            in_specs.append(pl.BlockSpec((tm, tn), lambda i, j, ob=ob: (i, j + ob)))
        else:
            in_specs.append(pl.BlockSpec((tm, tn), lambda i, j, k, ob=ob: (i, j + ob)))
        args.append(arr)
    if nk == 1:
        grid = (m // tm, n_out // tn)
        out_spec = pl.BlockSpec((tm, tn), lambda i, j: (i, j))
        scratch = []
    else:
        grid = (m // tm, n_out // tn, nk)
        out_spec = pl.BlockSpec((tm, tn), lambda i, j, k: (i, j))
        scratch = [pltpu.VMEM((tm, tn), F32)]
    return pl.pallas_call(
        functools.partial(_matmul_kernel, n_pairs=len(pairs), n_extra=len(extras), epilogue=epilogue, nk=nk),
        grid=grid,
        in_specs=in_specs,
        out_specs=out_spec,
        out_shape=jax.ShapeDtypeStruct((m, n_out), out_dtype),
        scratch_shapes=scratch,
        compiler_params=_params(len(grid)),
        name=name,
    )(*args)


def _ep_plain(accs, extras):
    return accs[0]


def _ep_residual(accs, extras):
    return extras[0] + accs[0]


def _ep_gated_merge(accs, extras):
    return jax.nn.sigmoid(extras[0]) * accs[0] + jax.nn.sigmoid(extras[1]) * accs[1]


def _ep_ple(accs, extras):
    return extras[0] + jax.nn.sigmoid(accs[0]) * accs[1]


def _rope_mix(x, cos, sin):
    return x * cos + pltpu.roll(x, A_ROPE // 2, 1) * sin


def _mla_prep_kernel(cq_ref, ckv_ref, kr_ref, gq_ref, gkv_ref, cos_ref, sin_ref,
                     cqn_ref, ckv_out_ref, ckvb_ref, krope_ref):
    cq = cq_ref[...]
    cqn = cq * lax.rsqrt(jnp.mean(cq * cq, axis=-1, keepdims=True) + EPS) * gq_ref[...]
    cqn_ref[...] = cqn.astype(cqn_ref.dtype)
    ckv = ckv_ref[...]
    ckvn = ckv * lax.rsqrt(jnp.mean(ckv * ckv, axis=-1, keepdims=True) + EPS) * gkv_ref[...]
    ckv_out_ref[...] = ckvn
    ckvb_ref[...] = ckvn.astype(ckvb_ref.dtype)
    krope_ref[...] = _rope_mix(kr_ref[...], cos_ref[...], sin_ref[...])[:, :A_ROPE]


def mla_prep(proj, lay, g_qa, g_kva, cos, sin):
    m = proj.shape[0]
    tm = _tile(m, 512, 8)
    (o_cq, w_cq), (o_ckv, w_ckv), (o_kr, w_kr) = lay["cq"], lay["ckv"], lay["kr2"]
    assert o_cq % w_cq == 0 and o_ckv % w_ckv == 0 and o_kr % w_kr == 0
    return pl.pallas_call(
        _mla_prep_kernel,
        grid=(m // tm,),
        in_specs=[pl.BlockSpec((tm, w_cq), lambda i: (i, o_cq // w_cq)),
                  pl.BlockSpec((tm, w_ckv), lambda i: (i, o_ckv // w_ckv)),
                  pl.BlockSpec((tm, w_kr), lambda i: (i, o_kr // w_kr)),
                  pl.BlockSpec((1, w_cq), lambda i: (0, 0)),
                  pl.BlockSpec((1, w_ckv), lambda i: (0, 0)),
                  pl.BlockSpec((tm, LANES), lambda i: (i, 0)),
                  pl.BlockSpec((tm, LANES), lambda i: (i, 0))],
        out_specs=[pl.BlockSpec((tm, w_cq), lambda i: (i, 0)),
                   pl.BlockSpec((tm, w_ckv), lambda i: (i, 0)),
                   pl.BlockSpec((tm, w_ckv), lambda i: (i, 0)),
                   pl.BlockSpec((tm, A_ROPE), lambda i: (i, 0))],
        out_shape=[jax.ShapeDtypeStruct((m, w_cq), BF16),
                   jax.ShapeDtypeStruct((m, w_ckv), F32),
                   jax.ShapeDtypeStruct((m, w_ckv), BF16),
                   jax.ShapeDtypeStruct((m, A_ROPE), F32)],
        compiler_params=_params(1),
        name="mla_prep",
    )(proj, proj, proj, g_qa.reshape(1, -1), g_kva.reshape(1, -1), cos, sin)


def _q_prep_kernel(x_ref, w_ref, cos_ref, sin_ref, *rest, absorbed):
    acc = _dot(x_ref[...], w_ref[...])
    nope = (acc[:, :A_NOPE] * A_SCALE).astype(BF16)
    rot = _rope_mix(acc[:, A_NOPE:], cos_ref[...], sin_ref[...])[:, :A_ROPE] * A_SCALE
    if absorbed:
        wuk_ref, o_ref = rest
        o_ref[:, :KV_LORA] = _dot(nope, wuk_ref[...]).astype(o_ref.dtype)
        o_ref[:, KV_LORA:] = rot.astype(o_ref.dtype)
    else:
        (o_ref,) = rest
        o_ref[:, :A_NOPE] = nope
        o_ref[:, A_NOPE:] = rot.astype(o_ref.dtype)


def q_prep(cqn, wq, cos, sin, wuk_t=None):
    m, kq = cqn.shape
    tm = _tile(m, ROW_TILE, 8)
    wcols = wq.shape[2]
    absorbed = wuk_t is not None
    width = (KV_LORA if absorbed else A_NOPE) + A_ROPE
    in_specs = [pl.BlockSpec((tm, kq), lambda h, i: (i, 0)),
                pl.BlockSpec((None, kq, wcols), lambda h, i: (h, 0, 0)),
                pl.BlockSpec((tm, LANES), lambda h, i: (i, 0)),
                pl.BlockSpec((tm, LANES), lambda h, i: (i, 0))]
    args = [cqn, wq, cos, sin]
    if absorbed:
        in_specs.append(pl.BlockSpec((None, A_NOPE, KV_LORA), lambda h, i: (h, 0, 0)))
        args.append(wuk_t)
    return pl.pallas_call(
        functools.partial(_q_prep_kernel, absorbed=absorbed),
        grid=(A_HEADS, m // tm),
        in_specs=in_specs,
        out_specs=pl.BlockSpec((None, tm, width), lambda h, i: (h, i, 0)),
        out_shape=jax.ShapeDtypeStruct((A_HEADS, m, width), BF16),
        compiler_params=_params(2),
        name="q_prep",
    )(*args)


def _kv_prep_kernel(x_ref, wk_ref, wvt_ref, kr_ref, k_ref, vt_ref, *, blk):
    x = x_ref[...]
    k_ref[:, :A_NOPE] = _dot(x, wk_ref[...]).astype(k_ref.dtype)
    k_ref[:, A_NOPE:] = kr_ref[...].astype(k_ref.dtype)
    vt = _dot_nt(wvt_ref[...], x).astype(vt_ref.dtype)
    for c in range(x.shape[0] // blk):
        vt_ref[c, :A_VDIM, :] = vt[:, c * blk:(c + 1) * blk]
        vt_ref[c, A_VDIM:, :] = jnp.ones((A_VDIM, blk), vt_ref.dtype)


def kv_prep(ckv_b, krope, wk, wvt, blk):
    m = ckv_b.shape[0]
    tm = _tile(m, ROW_TILE, blk)
    return pl.pallas_call(
        functools.partial(_kv_prep_kernel, blk=blk),
        grid=(A_HEADS, m // tm),
        in_specs=[pl.BlockSpec((tm, KV_LORA), lambda h, i: (i, 0)),
                  pl.BlockSpec((None, KV_LORA, A_NOPE), lambda h, i: (h, 0, 0)),
                  pl.BlockSpec((None, A_VDIM, KV_LORA), lambda h, i: (h, 0, 0)),
                  pl.BlockSpec((tm, A_ROPE), lambda h, i: (i, 0))],
        out_specs=[pl.BlockSpec((None, tm, A_NOPE + A_ROPE), lambda h, i: (h, i, 0)),
                   pl.BlockSpec((None, tm // blk, 2 * A_VDIM, blk), lambda h, i: (h, i, 0, 0))],
        out_shape=[jax.ShapeDtypeStruct((A_HEADS, m, A_NOPE + A_ROPE), BF16),
                   jax.ShapeDtypeStruct((A_HEADS, m // blk, 2 * A_VDIM, blk), BF16)],
        compiler_params=_params(2),
        name="kv_prep",
    )(ckv_b, wk, wvt, krope)


def _softmax_step(s, v, m, l, acc):
    m_new = jnp.maximum(m, jnp.max(s, axis=1, keepdims=True))
    p = jnp.exp(s - m_new)
    alpha = jnp.exp(m - m_new)
    l = alpha * l + jnp.sum(p, axis=1, keepdims=True)
    acc = alpha * acc + _dot(p.astype(BF16), v)
    return m_new, l, acc


def _flash_kernel(q_ref, k_ref, vt_ref, o_ref, m_ref, acc_ref, *, blk, hps):
    qi = pl.program_id(2)
    m_ref[...] = jnp.full_like(m_ref, -jnp.inf)
    acc_ref[...] = jnp.zeros_like(acc_ref)

    def block(kj, diagonal):
        start = pl.multiple_of(kj * blk, blk)
        scores = [_dot_nt(k_ref[h, pl.ds(start, blk), :], q_ref[h]) for h in range(hps)]
        if diagonal:
            key = lax.broadcasted_iota(jnp.int32, (blk, blk), 0)
            qry = lax.broadcasted_iota(jnp.int32, (blk, blk), 1)
            scores = [jnp.where(key <= qry, s, -jnp.inf) for s in scores]
        probs, alphas = [], []
        for h, s in enumerate(scores):
            m_old = m_ref[h]
            m_new = jnp.maximum(m_old, jnp.max(s, axis=0, keepdims=True))
            probs.append(jnp.exp(s - m_new).astype(BF16))
            alphas.append(jnp.exp(m_old - m_new))
            m_ref[h] = m_new
        for h in range(hps):
            acc_ref[h] = alphas[h] * acc_ref[h] + _dot(vt_ref[h, kj], probs[h])

    def body(kj, carry):
        block(kj, False)
        return carry

    lax.fori_loop(0, qi, body, 0)
    block(qi, True)
    for h in range(hps):
        acc = acc_ref[h]
        out_t = acc[:A_VDIM] / acc[A_VDIM:A_VDIM + 1]
        o_ref[:, h * A_VDIM:(h + 1) * A_VDIM] = out_t.T.astype(o_ref.dtype)


def flash_prompt(q, k, vt, batch, seq, blk):
    nq = seq // blk
    dqk = q.shape[2]
    hps = ATT_HEADS_PER_STEP
    assert A_HEADS % hps == 0
    return pl.pallas_call(
        functools.partial(_flash_kernel, blk=blk, hps=hps),
        grid=(batch, A_HEADS // hps, nq),
        in_specs=[pl.BlockSpec((hps, blk, dqk), lambda b, h, i: (h, b * nq + i, 0)),
                  pl.BlockSpec((hps, seq, dqk), lambda b, h, i: (h, b, 0)),
                  pl.BlockSpec((hps, nq, 2 * A_VDIM, blk), lambda b, h, i: (h, b, 0, 0))],
        out_specs=pl.BlockSpec((blk, hps * A_VDIM), lambda b, h, i: (b * nq + i, h)),
        out_shape=jax.ShapeDtypeStruct((batch * seq, A_HEADS * A_VDIM), BF16),
        scratch_shapes=[pltpu.VMEM((hps, 1, blk), F32), pltpu.VMEM((hps, 2 * A_VDIM, blk), F32)],
        compiler_params=_params(3),
        name="flash_prompt",
    )(q, k, vt)


def _paged_kernel(pt_ref, q_ref, *refs, n_steps, dec_seq):
    pp = PAGES_PER_STEP
    ckv_refs, kr_refs = refs[:pp], refs[pp:2 * pp]
    ckv_new_ref, kr_new_ref, o_ref, m_ref, l_ref, acc_ref = refs[2 * pp:]
    g = pl.program_id(1)

    @pl.when(g == 0)
    def _():
        m_ref[...] = jnp.full_like(m_ref, -jnp.inf)
        l_ref[...] = jnp.zeros_like(l_ref)
        acc_ref[...] = jnp.zeros_like(acc_ref)

    q = q_ref[...]
    q_lat, q_rope = q[:, :KV_LORA], q[:, KV_LORA:]
    grp = PAGES_PER_GROUP
    kcs, scores = [], []
    for lo in range(0, pp, grp):
        kc = jnp.concatenate([r[...].astype(BF16) for r in ckv_refs[lo:lo + grp]], axis=0)
        krt = jnp.concatenate([r[...].astype(BF16) for r in kr_refs[lo:lo + grp]], axis=1)
        kcs.append(kc)
        scores.append(_dot_nt(q_lat, kc) + _dot(q_rope, krt))
    m_old = m_ref[...]
    m_new = m_old
    for s in scores:
        m_new = jnp.maximum(m_new, jnp.max(s, axis=1, keepdims=True))
    alpha = jnp.exp(m_old - m_new)
    l = alpha * l_ref[...]
    acc = alpha * acc_ref[...]
    for s, kc in zip(scores, kcs):
        p = jnp.exp(s - m_new)
        l = l + jnp.sum(p, axis=1, keepdims=True)
        acc = acc + _dot(p.astype(BF16), kc)
    m_ref[...] = m_new
    l_ref[...] = l
    acc_ref[...] = acc

    @pl.when(g == n_steps - 1)
    def _():
        def pad_rows(x):
            return jnp.concatenate([x, jnp.zeros((PAGE_SIZE - dec_seq, x.shape[1]), x.dtype)], axis=0)

        kcn = pad_rows(ckv_new_ref[...]).astype(BF16)
        krn = pad_rows(kr_new_ref[...]).astype(BF16)
        sn = _dot_nt(q_lat, kcn) + _dot_nt(q_rope, krn)
        tok = lax.broadcasted_iota(jnp.int32, sn.shape, 0) % dec_seq
        key = lax.broadcasted_iota(jnp.int32, sn.shape, 1)
        sn = jnp.where(key <= tok, sn, -jnp.inf)
        _, l2, acc2 = _softmax_step(sn, kcn, m_ref[...], l_ref[...], acc_ref[...])
        o_ref[...] = (acc2 / l2).astype(o_ref.dtype)


def paged_attention(q, pool_ckv, pool_kr, page_table, ckv_new, kr_new):
    nb, rows, dq = q.shape
    n_pages = page_table.shape[1]
    dec_seq = ckv_new.shape[1]
    pp = PAGES_PER_STEP
    assert n_pages % pp == 0 and pp % PAGES_PER_GROUP == 0
    n_steps = n_pages // pp
    page = pool_ckv.shape[1]
    assert page == PAGE_SIZE and pool_kr.shape[1:] == (A_ROPE, page)

    def page_map(i):
        return lambda b, g, pt: (pt[b * n_pages + g * pp + i], 0, 0)

    in_specs = [pl.BlockSpec((None, rows, dq), lambda b, g, pt: (b, 0, 0))]
    in_specs += [pl.BlockSpec((None, page, KV_LORA), page_map(i)) for i in range(pp)]
    in_specs += [pl.BlockSpec((None, A_ROPE, page), page_map(i)) for i in range(pp)]
    in_specs += [pl.BlockSpec((None, dec_seq, KV_LORA), lambda b, g, pt: (b, 0, 0)),
                 pl.BlockSpec((None, dec_seq, A_ROPE), lambda b, g, pt: (b, 0, 0))]
    grid_spec = pltpu.PrefetchScalarGridSpec(
        num_scalar_prefetch=1,
        grid=(nb, n_steps),
        in_specs=in_specs,
        out_specs=pl.BlockSpec((None, rows, KV_LORA), lambda b, g, pt: (b, 0, 0)),
        scratch_shapes=[pltpu.VMEM((rows, 1), F32), pltpu.VMEM((rows, 1), F32),
                        pltpu.VMEM((rows, KV_LORA), F32)],
    )
    return pl.pallas_call(
        functools.partial(_paged_kernel, n_steps=n_steps, dec_seq=dec_seq),
        grid_spec=grid_spec,
        out_shape=jax.ShapeDtypeStruct((nb, rows, KV_LORA), BF16),
        compiler_params=_params(2),
        name="paged_attention",
    )(page_table.reshape(-1), q, *([pool_ckv] * pp), *([pool_kr] * pp), ckv_new, kr_new)


def _head_mm_kernel(x_ref, w_ref, o_ref):
    o_ref[...] = _dot(x_ref[...], w_ref[...]).astype(o_ref.dtype)


def head_matmul(x, w):
    nh, m, kd = x.shape
    n = w.shape[2]
    return pl.pallas_call(
        _head_mm_kernel,
        grid=(nh,),
        in_specs=[pl.BlockSpec((None, m, kd), lambda h: (h, 0, 0)),
                  pl.BlockSpec((None, kd, n), lambda h: (h, 0, 0))],
        out_specs=pl.BlockSpec((m, n), lambda h: (0, h)),
        out_shape=jax.ShapeDtypeStruct((m, nh * n), BF16),
        compiler_params=_params(1),
        name="head_matmul",
    )(x, w)


def _mlstm_kernel(q_ref, k_ref, v_ref, if_ref, mo_ref, bias_ref, gmh_ref, c0_ref, n0_ref, m0_ref,
                  o_ref, c_ref, n_ref, m_ref, *, rows, lp):
    chunk = pl.program_id(1)

    @pl.when(chunk == 0)
    def _():
        c_ref[...] = c0_ref[...]
        n_ref[...] = n0_ref[...]
        m_ref[...] = m0_ref[...]

    def pad(x):
        if rows == lp:
            return x
        return jnp.concatenate([x, jnp.zeros((lp - rows, x.shape[1]), x.dtype)], axis=0)

    t_idx = lax.broadcasted_iota(jnp.int32, (lp, lp), 0)
    s_idx = lax.broadcasted_iota(jnp.int32, (lp, lp), 1)
    causal = s_idx <= t_idx
    diag = s_idx == t_idx
    valid = lax.broadcasted_iota(jnp.int32, (lp, 1), 0) < rows
    gates = pad(if_ref[...] + bias_ref[...])

    for h in range(M_HEADS):
        q = pad(q_ref[:, h * M_DQK:(h + 1) * M_DQK]) * (M_DQK ** -0.5)
        k = pad(k_ref[:, h * M_DQK:(h + 1) * M_DQK])
        v = pad(v_ref[:, h * M_DV:(h + 1) * M_DV])
        i_col = gates[:, h:h + 1]
        f_pre = gates[:, M_HEADS + h:M_HEADS + h + 1]
        f_col = jnp.minimum(f_pre, 0.0) - jnp.log1p(jnp.exp(-jnp.abs(f_pre)))
        if rows != lp:
            i_col = jnp.where(valid, i_col, -jnp.inf)
            f_col = jnp.where(valid, f_col, 0.0)

        f_row = jnp.sum(jnp.where(diag, f_col, 0.0), axis=0, keepdims=True)
        i_row = jnp.sum(jnp.where(diag, i_col, 0.0), axis=0, keepdims=True)
        b_col = jnp.sum(jnp.where(causal, f_row, 0.0), axis=1, keepdims=True)
        b_row = jnp.sum(jnp.where(t_idx <= s_idx, f_col, 0.0), axis=0, keepdims=True)

        m_prev = m_ref[h]
        c_prev = c_ref[h]
        n_prev = n_ref[h]
        a_col = b_col + m_prev
        dmat = jnp.where(causal, b_col - b_row + i_row, -jnp.inf)
        mt = jnp.maximum(a_col, jnp.max(dmat, axis=1, keepdims=True))
        w_inter = jnp.exp(a_col - mt)
        qb, kb, vb = q.astype(BF16), k.astype(BF16), v.astype(BF16)
        qk = _dot_nt(qb, kb) * jnp.exp(dmat - mt)
        num = _dot(qk.astype(BF16), vb) + w_inter * _dot(qb, c_prev.astype(BF16))
        den = jnp.sum(qk, axis=1, keepdims=True) + w_inter * jnp.sum(q * n_prev, axis=1, keepdims=True)
        hid = num / jnp.maximum(jnp.abs(den), jnp.exp(-mt))

        b_last = b_col[lp - 1:lp, :]
        g_col = b_last - b_col + i_col
        m_new = jnp.maximum(b_last + m_prev, jnp.max(g_col, axis=0, keepdims=True))
        decay = jnp.exp(b_last + m_prev - m_new)
        kw = jnp.exp(g_col - m_new) * k
        c_ref[h] = decay * c_prev + _dot_tn(kw.astype(BF16), vb)
        n_ref[h] = decay * n_prev + jnp.sum(kw, axis=0, keepdims=True)
        m_ref[h] = m_new

        hn = hid * lax.rsqrt(jnp.mean(hid * hid, axis=1, keepdims=True) + EPS) * gmh_ref[h]
        out = jax.nn.sigmoid(pad(mo_ref[:, h * M_DV:(h + 1) * M_DV])) * hn
        o_ref[:, h * M_DV:(h + 1) * M_DV] = out[:rows].astype(o_ref.dtype)


def mlstm(proj, lay, bias_row, g_mh, c0, n0, m0, batch, seq):
    rows = _tile(seq, MLSTM_CHUNK, 8)
    lp = max(rows, MLSTM_MIN_ROWS)
    nc = seq // rows
    hq, hv = M_HEADS * M_DQK, M_HEADS * M_DV
    oq, ok, ov, oi, oo = (lay[n][0] for n in ("mq", "mk", "mv", "mif", "mo"))
    assert oq % hq == 0 and ok % hq == 0 and ov % hv == 0 and oo % hv == 0 and oi % LANES == 0

    def rowmap(col_block):
        return lambda b, c: (b * nc + c, col_block)

    def state_spec(*dims):
        return pl.BlockSpec((None, M_HEADS) + dims, lambda b, c: (b, 0, 0, 0))

    out, c1, n1, m1 = pl.pallas_call(
        functools.partial(_mlstm_kernel, rows=rows, lp=lp),
        grid=(batch, nc),
        in_specs=[pl.BlockSpec((rows, hq), rowmap(oq // hq)),
                  pl.BlockSpec((rows, hq), rowmap(ok // hq)),
                  pl.BlockSpec((rows, hv), rowmap(ov // hv)),
                  pl.BlockSpec((rows, LANES), rowmap(oi // LANES)),
                  pl.BlockSpec((rows, hv), rowmap(oo // hv)),
                  pl.BlockSpec((1, LANES), lambda b, c: (0, 0)),
                  pl.BlockSpec((M_HEADS, 1, M_DV), lambda b, c: (0, 0, 0)),
                  state_spec(M_DQK, M_DV), state_spec(1, M_DQK), state_spec(1, 1)],
        out_specs=[pl.BlockSpec((rows, hv), lambda b, c: (b * nc + c, 0)),
                   state_spec(M_DQK, M_DV), state_spec(1, M_DQK), state_spec(1, 1)],
        out_shape=[jax.ShapeDtypeStruct((batch * seq, hv), BF16),
                   jax.ShapeDtypeStruct((batch, M_HEADS, M_DQK, M_DV), F32),
                   jax.ShapeDtypeStruct((batch, M_HEADS, 1, M_DQK), F32),
                   jax.ShapeDtypeStruct((batch, M_HEADS, 1, 1), F32)],
        compiler_params=_params(2),
        name="mlstm",
    )(proj, proj, proj, proj, proj, bias_row, g_mh.reshape(M_HEADS, 1, M_DV),
      c0, n0.reshape(batch, M_HEADS, 1, M_DQK), m0.reshape(batch, M_HEADS, 1, 1))
    return out, c1, n1.reshape(batch, M_HEADS, M_DQK), m1.reshape(batch, M_HEADS)


def _convglu_kernel(g_ref, v_ref, c0_ref, w_ref, b_ref, act_ref, tail_ref):
    g = g_ref[...]
    seq = g.shape[1]
    c0 = c0_ref[...]
    t = lax.broadcasted_iota(jnp.int32, g.shape, 1)
    prev1 = jnp.where(t == 0, c0[:, 1:2, :], pltpu.roll(g, 1, 1))
    prev2 = jnp.where(t == 0, c0[:, 0:1, :], jnp.where(t == 1, c0[:, 1:2, :], pltpu.roll(g, 2, 1)))
    w = w_ref[...]
    conv = b_ref[...] + w[0:1, :] * prev2 + w[1:2, :] * prev1 + w[2:3, :] * g
    act_ref[...] = (jax.nn.gelu(conv, approximate=True) * v_ref[...]).astype(act_ref.dtype)
    tail_ref[...] = g[:, seq - (CONV_W - 1):, :]


def convglu(u, conv0, w_conv, b_conv, batch, seq):
    assert CONV_W == 3 and seq >= CONV_W - 1
    u3 = u.reshape(batch, seq, 2 * D_FF)
    tc = _tile(D_FF, 256)
    nb = max(1, min(batch, 1024 // seq))
    assert batch % nb == 0
    ncol = D_FF // tc
    act, tail = pl.pallas_call(
        _convglu_kernel,
        grid=(batch // nb, ncol),
        in_specs=[pl.BlockSpec((nb, seq, tc), lambda b, j: (b, 0, j)),
                  pl.BlockSpec((nb, seq, tc), lambda b, j: (b, 0, j + ncol)),
                  pl.BlockSpec((nb, CONV_W - 1, tc), lambda b, j: (b, 0, j)),
                  pl.BlockSpec((CONV_W, tc), lambda b, j: (0, j)),
                  pl.BlockSpec((1, tc), lambda b, j: (0, j))],
        out_specs=[pl.BlockSpec((nb, seq, tc), lambda b, j: (b, 0, j)),
                   pl.BlockSpec((nb, CONV_W - 1, tc), lambda b, j: (b, 0, j))],
        out_shape=[jax.ShapeDtypeStruct((batch, seq, D_FF), BF16),
                   jax.ShapeDtypeStruct((batch, CONV_W - 1, D_FF), F32)],
        compiler_params=_params(2),
        name="convglu",
    )(u3, u3, conv0, w_conv, b_conv.reshape(1, D_FF))
    return act.reshape(batch * seq, D_FF), tail


def _prepare_weights(w):
    lay, total = _padded_layout()
    offs = np.concatenate([[0], np.cumsum([Q_LORA, KV_LORA, A_ROPE, M_HEADS * M_DQK, M_HEADS * M_DQK,
                                            M_HEADS * M_DV, M_HEADS, M_HEADS, M_HEADS * M_DV, D_MODEL, D_MODEL])])
    src = {n: (int(offs[i]), int(offs[i + 1])) for i, n in enumerate(
        ["cq", "ckv", "kr", "mq", "mk", "mv", "mi", "mf", "mo", "ga", "gb"])}
    w_in = w["w_in"]

    def cols(name):
        return w_in[:, src[name][0]:src[name][1]]

    pieces, off = [], 0
    def put(name, block):
        nonlocal off
        o, wd = lay[name]
        if o > off:
            pieces.append(jnp.zeros((D_MODEL, o - off), w_in.dtype))
        pieces.append(block)
        if block.shape[1] < wd:
            pieces.append(jnp.zeros((D_MODEL, wd - block.shape[1]), w_in.dtype))
        off = o + wd

    for n in ("cq", "mq", "mk", "ckv"):
        put(n, cols(n))
    put("kr2", jnp.concatenate([cols("kr"), cols("kr")], axis=1))
    put("mif", jnp.concatenate([cols("mi"), cols("mf")], axis=1))
    for n in ("mv", "mo", "ga", "gb"):
        put(n, cols(n))
    assert off == total
    w_in_p = jnp.concatenate(pieces, axis=1).astype(BF16)

    w_uq = w["w_uq"]
    rope_cols = w_uq[..., A_NOPE:]
    wq = jnp.concatenate([w_uq[..., :A_NOPE], rope_cols, rope_cols], axis=-1).transpose(1, 0, 2).astype(BF16)
    bias_row = jnp.concatenate([w["b_i"].astype(F32), w["b_f"].astype(F32),
                                jnp.zeros((LANES - 2 * M_HEADS,), F32)]).reshape(1, LANES)
    return {
        "lay": lay, "n_proj": total, "w_in_p": w_in_p, "wq": wq,
        "wuk_t": w["w_uk"].transpose(1, 2, 0).astype(BF16),
        "wk": w["w_uk"].transpose(1, 0, 2).astype(BF16),
        "wv": w["w_uv"].transpose(1, 0, 2).astype(BF16),
        "wvt": w["w_uv"].transpose(1, 2, 0).astype(BF16),
        "bias_row": bias_row,
        "w_br_a": w["w_br_a"].astype(BF16), "w_br_b": w["w_br_b"].astype(BF16),
        "w_o": w["w_o"].astype(BF16), "w_up": w["w_up"].astype(BF16), "w_down": w["w_down"].astype(BF16),
        "w_ple_gate": w["w_ple_gate"].astype(BF16), "w_ple_proj": w["w_ple_proj"].astype(BF16),
    }


def _rope_tables(pos, batch):
    half = A_ROPE // 2
    inv = jnp.power(ROPE_THETA, -jnp.arange(half, dtype=F32) / half)
    ang = pos[:, None] * inv[None, :]
    cos, sin = jnp.cos(ang), jnp.sin(ang)
    reps = LANES // A_ROPE
    cos_t = jnp.tile(jnp.concatenate([cos, cos], axis=1), (batch, reps))
    sin_t = jnp.tile(jnp.concatenate([-sin, sin], axis=1), (batch, reps))
    return cos_t, sin_t


def _hybrid_layer(x, pe, pos, w, pw, c0, n0, m0, conv0, paged):
    batch, seq, _ = x.shape
    m = batch * seq
    lay = pw["lay"]
    x2 = x.reshape(m, D_MODEL)
    cos, sin = _rope_tables(pos, batch)

    h = rmsnorm_rows(x2, w["g_mix"], BF16)
    proj = fused_matmul([(h, pw["w_in_p"])], [], _ep_plain, pw["n_proj"], F32,
                        tn=_tile(pw["n_proj"], 1024), name="in_proj")
    cqn, c_kv, ckv_b, k_rope = mla_prep(proj, lay, w["g_qa"], w["g_kva"], cos, sin)

    if paged is None:
        q = q_prep(cqn, pw["wq"], cos, sin)
        blk = _tile(seq, ATT_BLOCK)
        k, vt = kv_prep(ckv_b, k_rope, pw["wk"], pw["wvt"], blk)
        a_out = flash_prompt(q, k, vt, batch, seq, blk)
    else:
        pool_ckv, pool_kr, page_table = paged
        q = q_prep(cqn, pw["wq"], cos, sin, pw["wuk_t"])
        dq = q.shape[2]
        q = q.reshape(A_HEADS, batch, seq, dq).transpose(1, 0, 2, 3).reshape(batch, A_HEADS * seq, dq)
        o_lat = paged_attention(q, pool_ckv, pool_kr, page_table,
                                c_kv.reshape(batch, seq, KV_LORA), k_rope.reshape(batch, seq, A_ROPE))
        o_lat = o_lat.reshape(batch, A_HEADS, seq, KV_LORA).transpose(1, 0, 2, 3).reshape(A_HEADS, m, KV_LORA)
        a_out = head_matmul(o_lat, pw["wv"])

    b_out, c1, n1, m1 = mlstm(proj, lay, pw["bias_row"], w["g_mh"], c0, n0, m0, batch, seq)

    merged = fused_matmul([(a_out, pw["w_br_a"]), (b_out, pw["w_br_b"])],
                          [(proj, lay["ga"][0]), (proj, lay["gb"][0])],
                          _ep_gated_merge, D_MODEL, BF16, name="branch_merge")
    x2 = fused_matmul([(merged, pw["w_o"])], [(x2, 0)], _ep_residual, D_MODEL, F32, name="out_proj")

    hf = rmsnorm_rows(x2, w["g_ffn"], BF16)
    u = fused_matmul([(hf, pw["w_up"])], [], _ep_plain, 2 * D_FF, F32, tn=_tile(2 * D_FF, 512), name="ffn_up")
    act, conv_tail = convglu(u, conv0.astype(F32), w["w_conv"], w["b_conv"], batch, seq)
    x2 = fused_matmul([(act, pw["w_down"])], [(x2, 0)], _ep_residual, D_MODEL, F32,
                      tn=_tile(D_MODEL, 512), tk=_tile(D_FF, D_FF // 2), name="ffn_down")

    hp = rmsnorm_rows(x2, w["g_ple"], BF16)
    x2 = fused_matmul([(hp, pw["w_ple_gate"]), (pe.reshape(m, PLE_DIM).astype(BF16), pw["w_ple_proj"])],
                      [(x2, 0)], _ep_ple, D_MODEL, F32, name="ple")
    return (x2, c_kv.reshape(batch, seq, KV_LORA), k_rope.reshape(batch, seq, A_ROPE), c1, n1, m1, conv_tail)


def kernel(x_prompt, x_sample, cache_ckv, cache_krope, state_C, state_n, state_m, state_conv, page_table,
           p_prompt, p_sample, g_mix, w_in, b_i, b_f, g_qa, w_uq, g_kva, w_uk, w_uv, g_mh, w_br_a, w_br_b,
           w_o, g_ffn, w_up, w_conv, b_conv, w_down, g_ple, w_ple_gate, w_ple_proj, g_final):
    bp, sp = x_prompt.shape[0], x_prompt.shape[1]
    bs, ss = x_sample.shape[0], x_sample.shape[1]
    depth = w_in.shape[0]
    past = page_table.shape[1] * cache_ckv.shape[2]
    pos_p = jnp.arange(sp, dtype=F32)
    pos_s = past + jnp.arange(ss, dtype=F32)
    c0p = jnp.zeros((bp, M_HEADS, M_DQK, M_DV), F32)
    n0p = jnp.zeros((bp, M_HEADS, M_DQK), F32)
    m0p = jnp.zeros((bp, M_HEADS), F32)
    conv0p = jnp.zeros((bp, CONV_W - 1, D_FF), F32)
    names = ["g_mix", "w_in", "b_i", "b_f", "g_qa", "w_uq", "g_kva", "w_uk", "w_uv", "g_mh", "w_br_a", "w_br_b",
             "w_o", "g_ffn", "w_up", "w_conv", "b_conv", "w_down", "g_ple", "w_ple_gate", "w_ple_proj"]
    stacked = [g_mix, w_in, b_i, b_f, g_qa, w_uq, g_kva, w_uk, w_uv, g_mh, w_br_a, w_br_b,
               w_o, g_ffn, w_up, w_conv, b_conv, w_down, g_ple, w_ple_gate, w_ple_proj]
    xp, xs = x_prompt, x_sample
    outs_p = [[] for _ in range(6)]
    outs_s = [[] for _ in range(6)]
    for l in range(depth):
        w = {n: a[l] for n, a in zip(names, stacked)}
        pw = _prepare_weights(w)
        xp2, *new_p = _hybrid_layer(xp, p_prompt[l], pos_p, w, pw, c0p, n0p, m0p, conv0p, None)
        xs2, *new_s = _hybrid_layer(xs, p_sample[l], pos_s, w, pw, state_C[l], state_n[l], state_m[l],
                                    state_conv[l],
                                    (cache_ckv[l], jnp.swapaxes(cache_krope[l], 1, 2), page_table))
        xp, xs = xp2.reshape(xp.shape), xs2.reshape(xs.shape)
        for j in range(6):
            outs_p[j].append(new_p[j])
            outs_s[j].append(new_s[j])
    stack_p = [jnp.stack(o) for o in outs_p]
    stack_s = [jnp.stack(o) for o in outs_s]
    y_prompt = rmsnorm_rows(xp.reshape(bp * sp, D_MODEL), g_final, F32).reshape(xp.shape)
    y_sample = rmsnorm_rows(xs.reshape(bs * ss, D_MODEL), g_final, F32).reshape(xs.shape)
    return (y_prompt, y_sample, *stack_p, *stack_s)
```

```python
import functools

import jax
import jax.numpy as jnp
import numpy as np
from jax import lax
from jax.experimental import pallas as pl
from jax.experimental.pallas import tpu as pltpu

D_MODEL = 4096
DEPTH = 1
PAGE_SIZE = 128
A_HEADS = 16
A_NOPE = 128
A_ROPE = 64
A_VDIM = 128
Q_LORA = 1024
KV_LORA = 512
ROPE_THETA = 10000.0
A_SCALE = (A_NOPE + A_ROPE) ** -0.5
M_HEADS = 4
M_DQK = 256
M_DV = 512
D_FF = 11008
CONV_W = 3
PLE_DIM = 256
EPS = 1e-6

LANES = 128
VMEM_LIMIT = 56 * 1024 * 1024
ROW_TILE = 1024
COL_TILE = 512
MLSTM_CHUNK = 256
MLSTM_MIN_ROWS = 128
ATT_BLOCK = 256
ATT_HEADS_PER_STEP = 4
PAGES_PER_STEP = 32
PAGES_PER_GROUP = 8

F32 = jnp.float32
BF16 = jnp.bfloat16


def _params(n_grid):
    return pltpu.CompilerParams(dimension_semantics=("arbitrary",) * n_grid,
                                vmem_limit_bytes=VMEM_LIMIT)


def _tile(n, pref, unit=LANES):
    if n <= pref:
        return n
    best = None
    for t in range(unit, pref + 1, unit):
        if n % t == 0:
            best = t
    assert best is not None, (n, pref)
    return best


def _padded_layout():
    hq, hv = M_HEADS * M_DQK, M_HEADS * M_DV
    widths = [("cq", Q_LORA), ("mq", hq), ("mk", hq), ("ckv", KV_LORA), ("kr2", 2 * A_ROPE), ("mif", LANES)]
    off, lay = 0, {}
    for name, w in widths:
        lay[name] = (off, w)
        off += w
    off = -(-off // hv) * hv
    for name, w in [("mv", hv), ("mo", hv), ("ga", D_MODEL), ("gb", D_MODEL)]:
        lay[name] = (off, w)
        off += w
    return lay, off


def _dot(a, b):
    return jnp.dot(a, b, preferred_element_type=F32)


def _dot_nt(a, b):
    return lax.dot_general(a, b, (((1,), (1,)), ((), ())), preferred_element_type=F32)


def _dot_tn(a, b):
    return lax.dot_general(a, b, (((0,), (0,)), ((), ())), preferred_element_type=F32)


def _rmsnorm_kernel(x_ref, g_ref, o_ref):
    x = x_ref[...]
    y = x * lax.rsqrt(jnp.mean(x * x, axis=-1, keepdims=True) + EPS)
    o_ref[...] = (y * g_ref[...]).astype(o_ref.dtype)


def rmsnorm_rows(x, g, out_dtype):
    m, d = x.shape
    tm = _tile(m, 512, 8)
    return pl.pallas_call(
        _rmsnorm_kernel,
        grid=(m // tm,),
        in_specs=[pl.BlockSpec((tm, d), lambda i: (i, 0)),
                  pl.BlockSpec((1, d), lambda i: (0, 0))],
        out_specs=pl.BlockSpec((tm, d), lambda i: (i, 0)),
        out_shape=jax.ShapeDtypeStruct((m, d), out_dtype),
        compiler_params=_params(1),
        name="rmsnorm",
    )(x, g.reshape(1, d).astype(F32))


def _matmul_kernel(*refs, n_pairs, n_extra, epilogue, nk):
    pairs = [(refs[2 * p], refs[2 * p + 1]) for p in range(n_pairs)]
    extras = refs[2 * n_pairs:2 * n_pairs + n_extra]
    o_ref = refs[2 * n_pairs + n_extra]
    if nk == 1:
        accs = [_dot(x[...], w[...]) for x, w in pairs]
        o_ref[...] = epilogue(accs, [e[...] for e in extras]).astype(o_ref.dtype)
        return
    acc_ref = refs[2 * n_pairs + n_extra + 1]
    k = pl.program_id(2)
    x, w = pairs[0]

    @pl.when(k == 0)
    def _():
        acc_ref[...] = jnp.zeros_like(acc_ref)

    acc_ref[...] += _dot(x[...], w[...])

    @pl.when(k == nk - 1)
    def _():
        o_ref[...] = epilogue([acc_ref[...]], [e[...] for e in extras]).astype(o_ref.dtype)


def fused_matmul(pairs, extras, epilogue, n_out, out_dtype, *, tm=None, tn=None, tk=None, name="matmul"):
    m = pairs[0][0].shape[0]
    tm = tm or _tile(m, ROW_TILE, 8)
    tn = tn or _tile(n_out, COL_TILE)
    kdim = pairs[0][0].shape[1]
    nk = 1 if tk is None else kdim // tk
    assert nk == 1 or len(pairs) == 1
    in_specs, args = [], []
    for x, w in pairs:
        kd = x.shape[1]
        if nk == 1:
            in_specs += [pl.BlockSpec((tm, kd), lambda i, j: (i, 0)),
                         pl.BlockSpec((kd, tn), lambda i, j: (0, j))]
        else:
            in_specs += [pl.BlockSpec((tm, tk), lambda i, j, k: (i, k)),
                         pl.BlockSpec((tk, tn), lambda i, j, k: (k, j))]
        args += [x, w]
    for arr, off in extras:
        assert off % tn == 0
        ob = off // tn
        if nk == 1:
            in_specs.append(pl.BlockSpec((tm, tn), lambda i, j, ob=ob: (i, j + ob)))
        else:
            in_specs.append(pl.BlockSpec((tm, tn), lambda i, j, k, ob=ob: (i, j + ob)))
        args.append(arr)
    if nk == 1:
        grid = (m // tm, n_out // tn)
        out_spec = pl.BlockSpec((tm, tn), lambda i, j: (i, j))
        scratch = []
    else:
        grid = (m // tm, n_out // tn, nk)
        out_spec = pl.BlockSpec((tm, tn), lambda i, j, k: (i, j))
        scratch = [pltpu.VMEM((tm, tn), F32)]
    return pl.pallas_call(
        functools.partial(_matmul_kernel, n_pairs=len(pairs), n_extra=len(extras), epilogue=epilogue, nk=nk),
        grid=grid,
        in_specs=in_specs,
        out_specs=out_spec,
        out_shape=jax.ShapeDtypeStruct((m, n_out), out_dtype),
        scratch_shapes=scratch,
        compiler_params=_params(len(grid)),
        name=name,
    )(*args)


def _ep_plain(accs, extras):
    return accs[0]


def _ep_residual(accs, extras):
    return extras[0] + accs[0]


def _ep_gated_merge(accs, extras):
    return jax.nn.sigmoid(extras[0]) * accs[0] + jax.nn.sigmoid(extras[1]) * accs[1]


def _ep_ple(accs, extras):
    return extras[0] + jax.nn.sigmoid(accs[0]) * accs[1]


def _rope_mix(x, cos, sin):
    return x * cos + pltpu.roll(x, A_ROPE // 2, 1) * sin


def _mla_prep_kernel(cq_ref, ckv_ref, kr_ref, gq_ref, gkv_ref, cos_ref, sin_ref,
                     cqn_ref, ckv_out_ref, ckvb_ref, krope_ref):
    cq = cq_ref[...]
    cqn = cq * lax.rsqrt(jnp.mean(cq * cq, axis=-1, keepdims=True) + EPS) * gq_ref[...]
    cqn_ref[...] = cqn.astype(cqn_ref.dtype)
    ckv = ckv_ref[...]
    ckvn = ckv * lax.rsqrt(jnp.mean(ckv * ckv, axis=-1, keepdims=True) + EPS) * gkv_ref[...]
    ckv_out_ref[...] = ckvn
    ckvb_ref[...] = ckvn.astype(ckvb_ref.dtype)
    krope_ref[...] = _rope_mix(kr_ref[...], cos_ref[...], sin_ref[...])[:, :A_ROPE]


def mla_prep(proj, lay, g_qa, g_kva, cos, sin):
    m = proj.shape[0]
    tm = _tile(m, 512, 8)
    (o_cq, w_cq), (o_ckv, w_ckv), (o_kr, w_kr) = lay["cq"], lay["ckv"], lay["kr2"]
    assert o_cq % w_cq == 0 and o_ckv % w_ckv == 0 and o_kr % w_kr == 0
    return pl.pallas_call(
        _mla_prep_kernel,
        grid=(m // tm,),
        in_specs=[pl.BlockSpec((tm, w_cq), lambda i: (i, o_cq // w_cq)),
                  pl.BlockSpec((tm, w_ckv), lambda i: (i, o_ckv // w_ckv)),
                  pl.BlockSpec((tm, w_kr), lambda i: (i, o_kr // w_kr)),
                  pl.BlockSpec((1, w_cq), lambda i: (0, 0)),
                  pl.BlockSpec((1, w_ckv), lambda i: (0, 0)),
                  pl.BlockSpec((tm, LANES), lambda i: (i, 0)),
                  pl.BlockSpec((tm, LANES), lambda i: (i, 0))],
        out_specs=[pl.BlockSpec((tm, w_cq), lambda i: (i, 0)),
                   pl.BlockSpec((tm, w_ckv), lambda i: (i, 0)),
                   pl.BlockSpec((tm, w_ckv), lambda i: (i, 0)),
                   pl.BlockSpec((tm, A_ROPE), lambda i: (i, 0))],
        out_shape=[jax.ShapeDtypeStruct((m, w_cq), BF16),
                   jax.ShapeDtypeStruct((m, w_ckv), F32),
                   jax.ShapeDtypeStruct((m, w_ckv), BF16),
                   jax.ShapeDtypeStruct((m, A_ROPE), F32)],
        compiler_params=_params(1),
        name="mla_prep",
    )(proj, proj, proj, g_qa.reshape(1, -1), g_kva.reshape(1, -1), cos, sin)


def _q_prep_kernel(x_ref, w_ref, cos_ref, sin_ref, *rest, absorbed):
    acc = _dot(x_ref[...], w_ref[...])
    nope = (acc[:, :A_NOPE] * A_SCALE).astype(BF16)
    rot = _rope_mix(acc[:, A_NOPE:], cos_ref[...], sin_ref[...])[:, :A_ROPE] * A_SCALE
    if absorbed:
        wuk_ref, o_ref = rest
        o_ref[:, :KV_LORA] = _dot(nope, wuk_ref[...]).astype(o_ref.dtype)
        o_ref[:, KV_LORA:] = rot.astype(o_ref.dtype)
    else:
        (o_ref,) = rest
        o_ref[:, :A_NOPE] = nope
        o_ref[:, A_NOPE:] = rot.astype(o_ref.dtype)


def q_prep(cqn, wq, cos, sin, wuk_t=None):
    m, kq = cqn.shape
    tm = _tile(m, ROW_TILE, 8)
    wcols = wq.shape[2]
    absorbed = wuk_t is not None
    width = (KV_LORA if absorbed else A_NOPE) + A_ROPE
    in_specs = [pl.BlockSpec((tm, kq), lambda h, i: (i, 0)),
                pl.BlockSpec((None, kq, wcols), lambda h, i: (h, 0, 0)),
                pl.BlockSpec((tm, LANES), lambda h, i: (i, 0)),
                pl.BlockSpec((tm, LANES), lambda h, i: (i, 0))]
    args = [cqn, wq, cos, sin]
    if absorbed:
        in_specs.append(pl.BlockSpec((None, A_NOPE, KV_LORA), lambda h, i: (h, 0, 0)))
        args.append(wuk_t)
    return pl.pallas_call(
        functools.partial(_q_prep_kernel, absorbed=absorbed),
        grid=(A_HEADS, m // tm),
        in_specs=in_specs,
        out_specs=pl.BlockSpec((None, tm, width), lambda h, i: (h, i, 0)),
        out_shape=jax.ShapeDtypeStruct((A_HEADS, m, width), BF16),
        compiler_params=_params(2),
        name="q_prep",
    )(*args)


def _kv_prep_kernel(x_ref, wk_ref, wvt_ref, kr_ref, k_ref, vt_ref, *, blk):
    x = x_ref[...]
    k_ref[:, :A_NOPE] = _dot(x, wk_ref[...]).astype(k_ref.dtype)
    k_ref[:, A_NOPE:] = kr_ref[...].astype(k_ref.dtype)
    vt = _dot_nt(wvt_ref[...], x).astype(vt_ref.dtype)
    for c in range(x.shape[0] // blk):
        vt_ref[c, :A_VDIM, :] = vt[:, c * blk:(c + 1) * blk]
        vt_ref[c, A_VDIM:, :] = jnp.ones((A_VDIM, blk), vt_ref.dtype)


def kv_prep(ckv_b, krope, wk, wvt, blk):
    m = ckv_b.shape[0]
    tm = _tile(m, ROW_TILE, blk)
    return pl.pallas_call(
        functools.partial(_kv_prep_kernel, blk=blk),
        grid=(A_HEADS, m // tm),
        in_specs=[pl.BlockSpec((tm, KV_LORA), lambda h, i: (i, 0)),
                  pl.BlockSpec((None, KV_LORA, A_NOPE), lambda h, i: (h, 0, 0)),
                  pl.BlockSpec((None, A_VDIM, KV_LORA), lambda h, i: (h, 0, 0)),
                  pl.BlockSpec((tm, A_ROPE), lambda h, i: (i, 0))],
        out_specs=[pl.BlockSpec((None, tm, A_NOPE + A_ROPE), lambda h, i: (h, i, 0)),
                   pl.BlockSpec((None, tm // blk, 2 * A_VDIM, blk), lambda h, i: (h, i, 0, 0))],
        out_shape=[jax.ShapeDtypeStruct((A_HEADS, m, A_NOPE + A_ROPE), BF16),
                   jax.ShapeDtypeStruct((A_HEADS, m // blk, 2 * A_VDIM, blk), BF16)],
        compiler_params=_params(2),
        name="kv_prep",
    )(ckv_b, wk, wvt, krope)


def _softmax_step(s, v, m, l, acc):
    m_new = jnp.maximum(m, jnp.max(s, axis=1, keepdims=True))
    p = jnp.exp(s - m_new)
    alpha = jnp.exp(m - m_new)
    l = alpha * l + jnp.sum(p, axis=1, keepdims=True)
    acc = alpha * acc + _dot(p.astype(BF16), v)
    return m_new, l, acc


def _flash_kernel(q_ref, k_ref, vt_ref, o_ref, m_ref, acc_ref, *, blk, hps):
    qi = pl.program_id(2)
    m_ref[...] = jnp.full_like(m_ref, -jnp.inf)
    acc_ref[...] = jnp.zeros_like(acc_ref)

    def block(kj, diagonal):
        start = pl.multiple_of(kj * blk, blk)
        scores = [_dot_nt(k_ref[h, pl.ds(start, blk), :], q_ref[h]) for h in range(hps)]
        if diagonal:
            key = lax.broadcasted_iota(jnp.int32, (blk, blk), 0)
            qry = lax.broadcasted_iota(jnp.int32, (blk, blk), 1)
            scores = [jnp.where(key <= qry, s, -jnp.inf) for s in scores]
        probs, alphas = [], []
        for h, s in enumerate(scores):
            m_old = m_ref[h]
            m_new = jnp.maximum(m_old, jnp.max(s, axis=0, keepdims=True))
            probs.append(jnp.exp(s - m_new).astype(BF16))
            alphas.append(jnp.exp(m_old - m_new))
            m_ref[h] = m_new
        for h in range(hps):
            acc_ref[h] = alphas[h] * acc_ref[h] + _dot(vt_ref[h, kj], probs[h])

    def body(kj, carry):
        block(kj, False)
        return carry

    lax.fori_loop(0, qi, body, 0)
    block(qi, True)
    for h in range(hps):
        acc = acc_ref[h]
        out_t = acc[:A_VDIM] / acc[A_VDIM:A_VDIM + 1]
        o_ref[:, h * A_VDIM:(h + 1) * A_VDIM] = out_t.T.astype(o_ref.dtype)


def flash_prompt(q, k, vt, batch, seq, blk):
    nq = seq // blk
    dqk = q.shape[2]
    hps = ATT_HEADS_PER_STEP
    assert A_HEADS % hps == 0
    return pl.pallas_call(
        functools.partial(_flash_kernel, blk=blk, hps=hps),
        grid=(batch, A_HEADS // hps, nq),
        in_specs=[pl.BlockSpec((hps, blk, dqk), lambda b, h, i: (h, b * nq + i, 0)),
                  pl.BlockSpec((hps, seq, dqk), lambda b, h, i: (h, b, 0)),
                  pl.BlockSpec((hps, nq, 2 * A_VDIM, blk), lambda b, h, i: (h, b, 0, 0))],
        out_specs=pl.BlockSpec((blk, hps * A_VDIM), lambda b, h, i: (b * nq + i, h)),
        out_shape=jax.ShapeDtypeStruct((batch * seq, A_HEADS * A_VDIM), BF16),
        scratch_shapes=[pltpu.VMEM((hps, 1, blk), F32), pltpu.VMEM((hps, 2 * A_VDIM, blk), F32)],
        compiler_params=_params(3),
        name="flash_prompt",
    )(q, k, vt)


def _paged_kernel(pt_ref, q_ref, *refs, n_steps, dec_seq):
    pp = PAGES_PER_STEP
    ckv_refs, kr_refs = refs[:pp], refs[pp:2 * pp]
    ckv_new_ref, kr_new_ref, o_ref, m_ref, l_ref, acc_ref = refs[2 * pp:]
    g = pl.program_id(1)

    @pl.when(g == 0)
    def _():
        m_ref[...] = jnp.full_like(m_ref, -jnp.inf)
        l_ref[...] = jnp.zeros_like(l_ref)
        acc_ref[...] = jnp.zeros_like(acc_ref)

    q = q_ref[...]
    q_lat, q_rope = q[:, :KV_LORA], q[:, KV_LORA:]
    grp = PAGES_PER_GROUP
    kcs, scores = [], []
    for lo in range(0, pp, grp):
        kc = jnp.concatenate([r[...].astype(BF16) for r in ckv_refs[lo:lo + grp]], axis=0)
        krt = jnp.concatenate([r[...].astype(BF16) for r in kr_refs[lo:lo + grp]], axis=1)
        kcs.append(kc)
        scores.append(_dot_nt(q_lat, kc) + _dot(q_rope, krt))
    m_old = m_ref[...]
    m_new = m_old
    for s in scores:
        m_new = jnp.maximum(m_new, jnp.max(s, axis=1, keepdims=True))
    alpha = jnp.exp(m_old - m_new)
    l = alpha * l_ref[...]
    acc = alpha * acc_ref[...]
    for s, kc in zip(scores, kcs):
        p = jnp.exp(s - m_new)
        l = l + jnp.sum(p, axis=1, keepdims=True)
        acc = acc + _dot(p.astype(BF16), kc)
    m_ref[...] = m_new
    l_ref[...] = l
    acc_ref[...] = acc

    @pl.when(g == n_steps - 1)
    def _():
        def pad_rows(x):
            return jnp.concatenate([x, jnp.zeros((PAGE_SIZE - dec_seq, x.shape[1]), x.dtype)], axis=0)

        kcn = pad_rows(ckv_new_ref[...]).astype(BF16)
        krn = pad_rows(kr_new_ref[...]).astype(BF16)
        sn = _dot_nt(q_lat, kcn) + _dot_nt(q_rope, krn)
        tok = lax.broadcasted_iota(jnp.int32, sn.shape, 0) % dec_seq
        key = lax.broadcasted_iota(jnp.int32, sn.shape, 1)
        sn = jnp.where(key <= tok, sn, -jnp.inf)
        _, l2, acc2 = _softmax_step(sn, kcn, m_ref[...], l_ref[...], acc_ref[...])
        o_ref[...] = (acc2 / l2).astype(o_ref.dtype)


def paged_attention(q, pool_ckv, pool_kr, page_table, ckv_new, kr_new):
    nb, rows, dq = q.shape
    n_pages = page_table.shape[1]
    dec_seq = ckv_new.shape[1]
    pp = PAGES_PER_STEP
    assert n_pages % pp == 0 and pp % PAGES_PER_GROUP == 0
    n_steps = n_pages // pp
    page = pool_ckv.shape[1]
    assert page == PAGE_SIZE and pool_kr.shape[1:] == (A_ROPE, page)

    def page_map(i):
        return lambda b, g, pt: (pt[b * n_pages + g * pp + i], 0, 0)

    in_specs = [pl.BlockSpec((None, rows, dq), lambda b, g, pt: (b, 0, 0))]
    in_specs += [pl.BlockSpec((None, page, KV_LORA), page_map(i)) for i in range(pp)]
    in_specs += [pl.BlockSpec((None, A_ROPE, page), page_map(i)) for i in range(pp)]
    in_specs += [pl.BlockSpec((None, dec_seq, KV_LORA), lambda b, g, pt: (b, 0, 0)),
                 pl.BlockSpec((None, dec_seq, A_ROPE), lambda b, g, pt: (b, 0, 0))]
    grid_spec = pltpu.PrefetchScalarGridSpec(
        num_scalar_prefetch=1,
        grid=(nb, n_steps),
        in_specs=in_specs,
        out_specs=pl.BlockSpec((None, rows, KV_LORA), lambda b, g, pt: (b, 0, 0)),
        scratch_shapes=[pltpu.VMEM((rows, 1), F32), pltpu.VMEM((rows, 1), F32),
                        pltpu.VMEM((rows, KV_LORA), F32)],
    )
    return pl.pallas_call(
        functools.partial(_paged_kernel, n_steps=n_steps, dec_seq=dec_seq),
        grid_spec=grid_spec,
        out_shape=jax.ShapeDtypeStruct((nb, rows, KV_LORA), BF16),
        compiler_params=_params(2),
        name="paged_attention",
    )(page_table.reshape(-1), q, *([pool_ckv] * pp), *([pool_kr] * pp), ckv_new, kr_new)


def _head_mm_kernel(x_ref, w_ref, o_ref):
    o_ref[...] = _dot(x_ref[...], w_ref[...]).astype(o_ref.dtype)


def head_matmul(x, w):
    nh, m, kd = x.shape
    n = w.shape[2]
    return pl.pallas_call(
        _head_mm_kernel,
        grid=(nh,),
        in_specs=[pl.BlockSpec((None, m, kd), lambda h: (h, 0, 0)),
                  pl.BlockSpec((None, kd, n), lambda h: (h, 0, 0))],
        out_specs=pl.BlockSpec((m, n), lambda h: (0, h)),
        out_shape=jax.ShapeDtypeStruct((m, nh * n), BF16),
        compiler_params=_params(1),
        name="head_matmul",
    )(x, w)


def _mlstm_kernel(q_ref, k_ref, v_ref, if_ref, mo_ref, bias_ref, gmh_ref, c0_ref, n0_ref, m0_ref,
                  o_ref, c_ref, n_ref, m_ref, *, rows, lp):
    chunk = pl.program_id(1)

    @pl.when(chunk == 0)
    def _():
        c_ref[...] = c0_ref[...]
        n_ref[...] = n0_ref[...]
        m_ref[...] = m0_ref[...]

    def pad(x):
        if rows == lp:
            return x
        return jnp.concatenate([x, jnp.zeros((lp - rows, x.shape[1]), x.dtype)], axis=0)

    t_idx = lax.broadcasted_iota(jnp.int32, (lp, lp), 0)
    s_idx = lax.broadcasted_iota(jnp.int32, (lp, lp), 1)
    causal = s_idx <= t_idx
    diag = s_idx == t_idx
    valid = lax.broadcasted_iota(jnp.int32, (lp, 1), 0) < rows
    gates = pad(if_ref[...] + bias_ref[...])

    for h in range(M_HEADS):
        q = pad(q_ref[:, h * M_DQK:(h + 1) * M_DQK]) * (M_DQK ** -0.5)
        k = pad(k_ref[:, h * M_DQK:(h + 1) * M_DQK])
        v = pad(v_ref[:, h * M_DV:(h + 1) * M_DV])
        i_col = gates[:, h:h + 1]
        f_pre = gates[:, M_HEADS + h:M_HEADS + h + 1]
        f_col = jnp.minimum(f_pre, 0.0) - jnp.log1p(jnp.exp(-jnp.abs(f_pre)))
        if rows != lp:
            i_col = jnp.where(valid, i_col, -jnp.inf)
            f_col = jnp.where(valid, f_col, 0.0)

        f_row = jnp.sum(jnp.where(diag, f_col, 0.0), axis=0, keepdims=True)
        i_row = jnp.sum(jnp.where(diag, i_col, 0.0), axis=0, keepdims=True)
        b_col = jnp.sum(jnp.where(causal, f_row, 0.0), axis=1, keepdims=True)
        b_row = jnp.sum(jnp.where(t_idx <= s_idx, f_col, 0.0), axis=0, keepdims=True)

        m_prev = m_ref[h]
        c_prev = c_ref[h]
        n_prev = n_ref[h]
        a_col = b_col + m_prev
        dmat = jnp.where(causal, b_col - b_row + i_row, -jnp.inf)
        mt = jnp.maximum(a_col, jnp.max(dmat, axis=1, keepdims=True))
        w_inter = jnp.exp(a_col - mt)
        qb, kb, vb = q.astype(BF16), k.astype(BF16), v.astype(BF16)
        qk = _dot_nt(qb, kb) * jnp.exp(dmat - mt)
        num = _dot(qk.astype(BF16), vb) + w_inter * _dot(qb, c_prev.astype(BF16))
        den = jnp.sum(qk, axis=1, keepdims=True) + w_inter * jnp.sum(q * n_prev, axis=1, keepdims=True)
        hid = num / jnp.maximum(jnp.abs(den), jnp.exp(-mt))

        b_last = b_col[lp - 1:lp, :]
        g_col = b_last - b_col + i_col
        m_new = jnp.maximum(b_last + m_prev, jnp.max(g_col, axis=0, keepdims=True))
        decay = jnp.exp(b_last + m_prev - m_new)
        kw = jnp.exp(g_col - m_new) * k
        c_ref[h] = decay * c_prev + _dot_tn(kw.astype(BF16), vb)
        n_ref[h] = decay * n_prev + jnp.sum(kw, axis=0, keepdims=True)
        m_ref[h] = m_new

        hn = hid * lax.rsqrt(jnp.mean(hid * hid, axis=1, keepdims=True) + EPS) * gmh_ref[h]
        out = jax.nn.sigmoid(pad(mo_ref[:, h * M_DV:(h + 1) * M_DV])) * hn
        o_ref[:, h * M_DV:(h + 1) * M_DV] = out[:rows].astype(o_ref.dtype)


def mlstm(proj, lay, bias_row, g_mh, c0, n0, m0, batch, seq):
    rows = _tile(seq, MLSTM_CHUNK, 8)
    lp = max(rows, MLSTM_MIN_ROWS)
    nc = seq // rows
    hq, hv = M_HEADS * M_DQK, M_HEADS * M_DV
    oq, ok, ov, oi, oo = (lay[n][0] for n in ("mq", "mk", "mv", "mif", "mo"))
    assert oq % hq == 0 and ok % hq == 0 and ov % hv == 0 and oo % hv == 0 and oi % LANES == 0

    def rowmap(col_block):
        return lambda b, c: (b * nc + c, col_block)

    def state_spec(*dims):
        return pl.BlockSpec((None, M_HEADS) + dims, lambda b, c: (b, 0, 0, 0))

    out, c1, n1, m1 = pl.pallas_call(
        functools.partial(_mlstm_kernel, rows=rows, lp=lp),
        grid=(batch, nc),
        in_specs=[pl.BlockSpec((rows, hq), rowmap(oq // hq)),
                  pl.BlockSpec((rows, hq), rowmap(ok // hq)),
                  pl.BlockSpec((rows, hv), rowmap(ov // hv)),
                  pl.BlockSpec((rows, LANES), rowmap(oi // LANES)),
                  pl.BlockSpec((rows, hv), rowmap(oo // hv)),
                  pl.BlockSpec((1, LANES), lambda b, c: (0, 0)),
                  pl.BlockSpec((M_HEADS, 1, M_DV), lambda b, c: (0, 0, 0)),
                  state_spec(M_DQK, M_DV), state_spec(1, M_DQK), state_spec(1, 1)],
        out_specs=[pl.BlockSpec((rows, hv), lambda b, c: (b * nc + c, 0)),
                   state_spec(M_DQK, M_DV), state_spec(1, M_DQK), state_spec(1, 1)],
        out_shape=[jax.ShapeDtypeStruct((batch * seq, hv), BF16),
                   jax.ShapeDtypeStruct((batch, M_HEADS, M_DQK, M_DV), F32),
                   jax.ShapeDtypeStruct((batch, M_HEADS, 1, M_DQK), F32),
                   jax.ShapeDtypeStruct((batch, M_HEADS, 1, 1), F32)],
        compiler_params=_params(2),
        name="mlstm",
    )(proj, proj, proj, proj, proj, bias_row, g_mh.reshape(M_HEADS, 1, M_DV),
      c0, n0.reshape(batch, M_HEADS, 1, M_DQK), m0.reshape(batch, M_HEADS, 1, 1))
    return out, c1, n1.reshape(batch, M_HEADS, M_DQK), m1.reshape(batch, M_HEADS)


def _ffn_up_kernel(x_ref, wg_ref, wv_ref, c0_ref, w_ref, b_ref, act_ref, tail_ref, *carry, nb, tiles_per_seq):
    i, j = pl.program_id(0), pl.program_id(1)
    x = x_ref[...]
    tm, tc = act_ref.shape
    g = _dot(x, wg_ref[...]).reshape(nb, tm // nb, tc)
    val = _dot(x, wv_ref[...]).reshape(nb, tm // nb, tc)
    st = tm // nb
    last_rows = g[:, st - (CONV_W - 1):, :]
    prev = c0_ref[...]
    if tiles_per_seq > 1:
        carry_ref = carry[0]
        prev = jnp.where(i % tiles_per_seq == 0, prev, carry_ref[j])
        carry_ref[j] = last_rows
    t = lax.broadcasted_iota(jnp.int32, g.shape, 1)
    prev1 = jnp.where(t == 0, prev[:, 1:2, :], pltpu.roll(g, 1, 1))
    prev2 = jnp.where(t == 0, prev[:, 0:1, :], jnp.where(t == 1, prev[:, 1:2, :], pltpu.roll(g, 2, 1)))
    w = w_ref[...]
    conv = b_ref[...] + w[0:1, :] * prev2 + w[1:2, :] * prev1 + w[2:3, :] * g
    act_ref[...] = (jax.nn.gelu(conv, approximate=True) * val).reshape(tm, tc).astype(act_ref.dtype)
    tail_ref[...] = last_rows


def ffn_up_convglu(x, w_up, conv0, w_conv, b_conv, batch, seq):
    assert CONV_W == 3 and seq >= CONV_W - 1
    m, kd = x.shape
    tc = _tile(D_FF, 256)
    ncol = D_FF // tc
    if seq >= ROW_TILE:
        tm, nb = _tile(seq, ROW_TILE, 8), 1
    else:
        nb = max(1, min(batch, ROW_TILE // seq))
        assert batch % nb == 0
        tm = nb * seq
    tps = seq // (tm // nb)
    scratch = [pltpu.VMEM((ncol, nb, CONV_W - 1, tc), F32)] if tps > 1 else []
    act, tail = pl.pallas_call(
        functools.partial(_ffn_up_kernel, nb=nb, tiles_per_seq=tps),
        grid=(m // tm, ncol),
        in_specs=[pl.BlockSpec((tm, kd), lambda i, j: (i, 0)),
                  pl.BlockSpec((kd, tc), lambda i, j: (0, j)),
                  pl.BlockSpec((kd, tc), lambda i, j: (0, j + ncol)),
                  pl.BlockSpec((nb, CONV_W - 1, tc), lambda i, j: (i // tps, 0, j)),
                  pl.BlockSpec((CONV_W, tc), lambda i, j: (0, j)),
                  pl.BlockSpec((1, tc), lambda i, j: (0, j))],
        out_specs=[pl.BlockSpec((tm, tc), lambda i, j: (i, j)),
                   pl.BlockSpec((nb, CONV_W - 1, tc), lambda i, j: (i, 0, j))],
        out_shape=[jax.ShapeDtypeStruct((m, D_FF), BF16),
                   jax.ShapeDtypeStruct((batch * tps, CONV_W - 1, D_FF), F32)],
        scratch_shapes=scratch,
        compiler_params=_params(2),
        name="ffn_up_convglu",
    )(x, w_up, w_up, conv0, w_conv, b_conv.reshape(1, D_FF))
    return act, tail.reshape(batch, tps, CONV_W - 1, D_FF)[:, tps - 1]


_MODE_COPY, _MODE_DUP_ROPE, _MODE_GATES, _MODE_ZERO = 0, 1, 2, 3


def _relayout_kernel(blk_ref, rot_ref, mode_ref, a_ref, b_ref, o_ref, *, rotations):
    t = pl.program_id(0)
    rot, mode = rot_ref[t], mode_ref[t]
    lane = lax.broadcasted_iota(jnp.int32, a_ref.shape, 1)

    def window(r):
        if r == 0:
            return a_ref[...]
        return jnp.where(lane < LANES - r, pltpu.roll(a_ref[...], LANES - r, 1), pltpu.roll(b_ref[...], LANES - r, 1))

    for r in rotations:
        @pl.when((rot == r) & (mode == _MODE_COPY))
        def _(r=r):
            o_ref[...] = window(r).astype(o_ref.dtype)

        @pl.when((rot == r) & (mode == _MODE_DUP_ROPE))
        def _(r=r):
            win = window(r)
            o_ref[...] = jnp.where(lane < A_ROPE, win, pltpu.roll(win, A_ROPE, 1)).astype(o_ref.dtype)

        @pl.when((rot == r) & (mode == _MODE_GATES))
        def _(r=r):
            o_ref[...] = jnp.where(lane < 2 * M_HEADS, window(r), 0.0).astype(o_ref.dtype)

    @pl.when(mode == _MODE_ZERO)
    def _():
        o_ref[...] = jnp.zeros_like(o_ref)


def relayout_w_in(w_in, lay, total, src_off):
    assert 2 * A_ROPE == LANES and 2 * M_HEADS <= LANES
    n_tiles = total // LANES
    blk, rot, mode = (np.zeros(n_tiles, np.int32) for _ in range(3))
    mode[:] = _MODE_ZERO
    groups = [(n, lay[n][0], lay[n][1], src_off[n], _MODE_COPY) for n in ("cq", "mq", "mk", "ckv", "mv", "mo", "ga", "gb")]
    groups += [("kr2", *lay["kr2"], src_off["kr"], _MODE_DUP_ROPE), ("mif", *lay["mif"], src_off["mi"], _MODE_GATES)]
    for _, dst, width, s0, md in groups:
        assert dst % LANES == 0 and width % LANES == 0
        for t in range(width // LANES):
            s = s0 + t * LANES
            blk[dst // LANES + t], rot[dst // LANES + t], mode[dst // LANES + t] = s // LANES, s % LANES, md
    rows, n_src = w_in.shape
    last = -(-n_src // LANES) - 1
    return pl.pallas_call(
        functools.partial(_relayout_kernel, rotations=tuple(sorted(set(int(r) for r in rot)))),
        grid_spec=pltpu.PrefetchScalarGridSpec(
            num_scalar_prefetch=3,
            grid=(n_tiles,),
            in_specs=[pl.BlockSpec((rows, LANES), lambda t, blk, rot, mode: (0, blk[t])),
                      pl.BlockSpec((rows, LANES), lambda t, blk, rot, mode: (0, jnp.minimum(blk[t] + 1, last)))],
            out_specs=pl.BlockSpec((rows, LANES), lambda t, blk, rot, mode: (0, t)),
        ),
        out_shape=jax.ShapeDtypeStruct((rows, total), BF16),
        compiler_params=_params(1),
        name="relayout_w_in",
    )(jnp.asarray(blk), jnp.asarray(rot), jnp.asarray(mode), w_in, w_in)


def _prepare_weights(w):
    lay, total = _padded_layout()
    offs = np.concatenate([[0], np.cumsum([Q_LORA, KV_LORA, A_ROPE, M_HEADS * M_DQK, M_HEADS * M_DQK,
                                            M_HEADS * M_DV, M_HEADS, M_HEADS, M_HEADS * M_DV, D_MODEL, D_MODEL])])
    src = {n: (int(offs[i]), int(offs[i + 1])) for i, n in enumerate(
        ["cq", "ckv", "kr", "mq", "mk", "mv", "mi", "mf", "mo", "ga", "gb"])}
    assert src["mf"][0] == src["mi"][1]
    w_in_p = relayout_w_in(w["w_in"], lay, total, {n: s[0] for n, s in src.items()})

    w_uq = w["w_uq"]
    rope_cols = w_uq[..., A_NOPE:]
    wq = jnp.concatenate([w_uq[..., :A_NOPE], rope_cols, rope_cols], axis=-1).transpose(1, 0, 2).astype(BF16)
    bias_row = jnp.concatenate([w["b_i"].astype(F32), w["b_f"].astype(F32),
                                jnp.zeros((LANES - 2 * M_HEADS,), F32)]).reshape(1, LANES)
    return {
        "lay": lay, "n_proj": total, "w_in_p": w_in_p, "wq": wq,
        "wuk_t": w["w_uk"].transpose(1, 2, 0).astype(BF16),
        "wk": w["w_uk"].transpose(1, 0, 2).astype(BF16),
        "wv": w["w_uv"].transpose(1, 0, 2).astype(BF16),
        "wvt": w["w_uv"].transpose(1, 2, 0).astype(BF16),
        "bias_row": bias_row,
        "w_br_a": w["w_br_a"].astype(BF16), "w_br_b": w["w_br_b"].astype(BF16),
        "w_o": w["w_o"].astype(BF16), "w_up": w["w_up"].astype(BF16), "w_down": w["w_down"].astype(BF16),
        "w_ple_gate": w["w_ple_gate"].astype(BF16), "w_ple_proj": w["w_ple_proj"].astype(BF16),
    }


def _rope_tables(pos, batch):
    half = A_ROPE // 2
    inv = jnp.power(ROPE_THETA, -jnp.arange(half, dtype=F32) / half)
    ang = pos[:, None] * inv[None, :]
    cos, sin = jnp.cos(ang), jnp.sin(ang)
    reps = LANES // A_ROPE
    cos_t = jnp.tile(jnp.concatenate([cos, cos], axis=1), (batch, reps))
    sin_t = jnp.tile(jnp.concatenate([-sin, sin], axis=1), (batch, reps))
    return cos_t, sin_t


def _hybrid_layer(x, pe, pos, w, pw, c0, n0, m0, conv0, paged):
    batch, seq, _ = x.shape
    m = batch * seq
    lay = pw["lay"]
    x2 = x.reshape(m, D_MODEL)
    cos, sin = _rope_tables(pos, batch)

    h = rmsnorm_rows(x2, w["g_mix"], BF16)
    proj = fused_matmul([(h, pw["w_in_p"])], [], _ep_plain, pw["n_proj"], F32,
                        tn=_tile(pw["n_proj"], 1024), name="in_proj")
    cqn, c_kv, ckv_b, k_rope = mla_prep(proj, lay, w["g_qa"], w["g_kva"], cos, sin)

    if paged is None:
        q = q_prep(cqn, pw["wq"], cos, sin)
        blk = _tile(seq, ATT_BLOCK)
        k, vt = kv_prep(ckv_b, k_rope, pw["wk"], pw["wvt"], blk)
        a_out = flash_prompt(q, k, vt, batch, seq, blk)
    else:
        pool_ckv, pool_kr, page_table = paged
        q = q_prep(cqn, pw["wq"], cos, sin, pw["wuk_t"])
        dq = q.shape[2]
        q = q.reshape(A_HEADS, batch, seq, dq).transpose(1, 0, 2, 3).reshape(batch, A_HEADS * seq, dq)
        o_lat = paged_attention(q, pool_ckv, pool_kr, page_table,
                                c_kv.reshape(batch, seq, KV_LORA), k_rope.reshape(batch, seq, A_ROPE))
        o_lat = o_lat.reshape(batch, A_HEADS, seq, KV_LORA).transpose(1, 0, 2, 3).reshape(A_HEADS, m, KV_LORA)
        a_out = head_matmul(o_lat, pw["wv"])

    b_out, c1, n1, m1 = mlstm(proj, lay, pw["bias_row"], w["g_mh"], c0, n0, m0, batch, seq)

    merged = fused_matmul([(a_out, pw["w_br_a"]), (b_out, pw["w_br_b"])],
                          [(proj, lay["ga"][0]), (proj, lay["gb"][0])],
                          _ep_gated_merge, D_MODEL, BF16, name="branch_merge")
    x2 = fused_matmul([(merged, pw["w_o"])], [(x2, 0)], _ep_residual, D_MODEL, F32, name="out_proj")

    hf = rmsnorm_rows(x2, w["g_ffn"], BF16)
    act, conv_tail = ffn_up_convglu(hf, pw["w_up"], conv0.astype(F32), w["w_conv"], w["b_conv"], batch, seq)
    x2 = fused_matmul([(act, pw["w_down"])], [(x2, 0)], _ep_residual, D_MODEL, F32,
                      tn=_tile(D_MODEL, 512), tk=_tile(D_FF, D_FF // 2), name="ffn_down")

    hp = rmsnorm_rows(x2, w["g_ple"], BF16)
    x2 = fused_matmul([(hp, pw["w_ple_gate"]), (pe.reshape(m, PLE_DIM).astype(BF16), pw["w_ple_proj"])],
                      [(x2, 0)], _ep_ple, D_MODEL, F32, name="ple")
    return (x2, c_kv.reshape(batch, seq, KV_LORA), k_rope.reshape(batch, seq, A_ROPE), c1, n1, m1, conv_tail)


def kernel(x_prompt, x_sample, cache_ckv, cache_krope, state_C, state_n, state_m, state_conv, page_table,
           p_prompt, p_sample, g_mix, w_in, b_i, b_f, g_qa, w_uq, g_kva, w_uk, w_uv, g_mh, w_br_a, w_br_b,
           w_o, g_ffn, w_up, w_conv, b_conv, w_down, g_ple, w_ple_gate, w_ple_proj, g_final):
    bp, sp = x_prompt.shape[0], x_prompt.shape[1]
    bs, ss = x_sample.shape[0], x_sample.shape[1]
    depth = w_in.shape[0]
    past = page_table.shape[1] * cache_ckv.shape[2]
    pos_p = jnp.arange(sp, dtype=F32)
    pos_s = past + jnp.arange(ss, dtype=F32)
    c0p = jnp.zeros((bp, M_HEADS, M_DQK, M_DV), F32)
    n0p = jnp.zeros((bp, M_HEADS, M_DQK), F32)
    m0p = jnp.zeros((bp, M_HEADS), F32)
    conv0p = jnp.zeros((bp, CONV_W - 1, D_FF), F32)
    names = ["g_mix", "w_in", "b_i", "b_f", "g_qa", "w_uq", "g_kva", "w_uk", "w_uv", "g_mh", "w_br_a", "w_br_b",
             "w_o", "g_ffn", "w_up", "w_conv", "b_conv", "w_down", "g_ple", "w_ple_gate", "w_ple_proj"]
    stacked = [g_mix, w_in, b_i, b_f, g_qa, w_uq, g_kva, w_uk, w_uv, g_mh, w_br_a, w_br_b,
               w_o, g_ffn, w_up, w_conv, b_conv, w_down, g_ple, w_ple_gate, w_ple_proj]
    xp, xs = x_prompt, x_sample
    outs_p = [[] for _ in range(6)]
    outs_s = [[] for _ in range(6)]
    for l in range(depth):
        w = {n: a[l] for n, a in zip(names, stacked)}
        pw = _prepare_weights(w)
        xp2, *new_p = _hybrid_layer(xp, p_prompt[l], pos_p, w, pw, c0p, n0p, m0p, conv0p, None)
        xs2, *new_s = _hybrid_layer(xs, p_sample[l], pos_s, w, pw, state_C[l], state_n[l], state_m[l],
                                    state_conv[l],
                                    (cache_ckv[l], jnp.swapaxes(cache_krope[l], 1, 2), page_table))
        xp, xs = xp2.reshape(xp.shape), xs2.reshape(xs.shape)
        for j in range(6):
            outs_p[j].append(new_p[j])
            outs_s[j].append(new_s[j])
    stack_p = [jnp.stack(o) for o in outs_p]
    stack_s = [jnp.stack(o) for o in outs_s]
    y_prompt = rmsnorm_rows(xp.reshape(bp * sp, D_MODEL), g_final, F32).reshape(xp.shape)
    y_sample = rmsnorm_rows(xs.reshape(bs * ss, D_MODEL), g_final, F32).reshape(xs.shape)
    return (y_prompt, y_sample, *stack_p, *stack_s)
```

```python
import functools

import jax
import jax.numpy as jnp
import numpy as np
from jax import lax
from jax.experimental import pallas as pl
from jax.experimental.pallas import tpu as pltpu

D_MODEL = 4096
DEPTH = 1
PAGE_SIZE = 128
A_HEADS = 16
A_NOPE = 128
A_ROPE = 64
A_VDIM = 128
Q_LORA = 1024
KV_LORA = 512
ROPE_THETA = 10000.0
A_SCALE = (A_NOPE + A_ROPE) ** -0.5
M_HEADS = 4
M_DQK = 256
M_DV = 512
D_FF = 11008
CONV_W = 3
PLE_DIM = 256
EPS = 1e-6

LANES = 128
VMEM_LIMIT = 56 * 1024 * 1024
ROW_TILE = 1024
COL_TILE = 512
MLSTM_CHUNK = 256
MLSTM_MIN_ROWS = 128
ATT_BLOCK = 256
ATT_HEADS_PER_STEP = 4
PREP_HEADS_PER_STEP = 4
PAGES_PER_STEP = 32
PAGES_PER_GROUP = 8

F32 = jnp.float32
BF16 = jnp.bfloat16


def _params(n_grid):
    return pltpu.CompilerParams(dimension_semantics=("arbitrary",) * n_grid,
                                vmem_limit_bytes=VMEM_LIMIT)


def _tile(n, pref, unit=LANES):
    if n <= pref:
        return n
    best = None
    for t in range(unit, pref + 1, unit):
        if n % t == 0:
            best = t
    assert best is not None, (n, pref)
    return best


def _padded_layout():
    hq, hv = M_HEADS * M_DQK, M_HEADS * M_DV
    widths = [("cq", Q_LORA), ("mq", hq), ("mk", hq), ("ckv", KV_LORA), ("kr2", 2 * A_ROPE), ("mif", LANES)]
    off, lay = 0, {}
    for name, w in widths:
        lay[name] = (off, w)
        off += w
    off = -(-off // hv) * hv
    for name, w in [("mv", hv), ("mo", hv), ("ga", D_MODEL), ("gb", D_MODEL)]:
        lay[name] = (off, w)
        off += w
    return lay, off


def _dot(a, b):
    return jnp.dot(a, b, preferred_element_type=F32)


def _dot_nt(a, b):
    return lax.dot_general(a, b, (((1,), (1,)), ((), ())), preferred_element_type=F32)


def _dot_tn(a, b):
    return lax.dot_general(a, b, (((0,), (0,)), ((), ())), preferred_element_type=F32)


def _rmsnorm_kernel(x_ref, g_ref, o_ref):
    x = x_ref[...]
    y = x * lax.rsqrt(jnp.mean(x * x, axis=-1, keepdims=True) + EPS)
    o_ref[...] = (y * g_ref[...]).astype(o_ref.dtype)


def rmsnorm_rows(x, g, out_dtype):
    m, d = x.shape
    tm = _tile(m, 512, 8)
    return pl.pallas_call(
        _rmsnorm_kernel,
        grid=(m // tm,),
        in_specs=[pl.BlockSpec((tm, d), lambda i: (i, 0)),
                  pl.BlockSpec((1, d), lambda i: (0, 0))],
        out_specs=pl.BlockSpec((tm, d), lambda i: (i, 0)),
        out_shape=jax.ShapeDtypeStruct((m, d), out_dtype),
        compiler_params=_params(1),
        name="rmsnorm",
    )(x, g.reshape(1, d).astype(F32))


def _cast_weights_once(w_refs, cast_refs):
    @pl.when(pl.program_id(1) == 0)
    def _():
        for w, c in zip(w_refs, cast_refs):
            c[...] = w[...].astype(c.dtype)


def _matmul_kernel(*refs, n_pairs, n_extra, epilogue, nk, n_cast):
    pairs = [(refs[2 * p], refs[2 * p + 1]) for p in range(n_pairs)]
    extras = refs[2 * n_pairs:2 * n_pairs + n_extra]
    o_ref = refs[2 * n_pairs + n_extra]
    if nk == 1:
        if n_cast:
            cast_refs = refs[2 * n_pairs + n_extra + 1:]
            _cast_weights_once([w for _, w in pairs], cast_refs)
            pairs = [(x, c) for (x, _), c in zip(pairs, cast_refs)]
        accs = [_dot(x[...], w[...]) for x, w in pairs]
        o_ref[...] = epilogue(accs, [e[...] for e in extras]).astype(o_ref.dtype)
        return
    acc_ref = refs[2 * n_pairs + n_extra + 1]
    k = pl.program_id(2)
    x, w = pairs[0]

    @pl.when(k == 0)
    def _():
        acc_ref[...] = jnp.zeros_like(acc_ref)

    acc_ref[...] += _dot(x[...], w[...])

    @pl.when(k == nk - 1)
    def _():
        o_ref[...] = epilogue([acc_ref[...]], [e[...] for e in extras]).astype(o_ref.dtype)


def fused_matmul(pairs, extras, epilogue, n_out, out_dtype, *, tm=None, tn=None, tk=None, name="matmul"):
    m = pairs[0][0].shape[0]
    tm = tm or _tile(m, ROW_TILE, 8)
    tn = tn or _tile(n_out, COL_TILE)
    kdim = pairs[0][0].shape[1]
    nk = 1 if tk is None else kdim // tk
    assert nk == 1 or len(pairs) == 1
    cast = any(w.dtype != BF16 for _, w in pairs)
    assert not cast or (nk == 1 and all(w.dtype == F32 for _, w in pairs))

    def order(f):
        return (lambda j, i: f(i, j)) if cast else f

    in_specs, args = [], []
    for x, w in pairs:
        kd = x.shape[1]
        if nk == 1:
            in_specs += [pl.BlockSpec((tm, kd), order(lambda i, j: (i, 0))),
                         pl.BlockSpec((kd, tn), order(lambda i, j: (0, j)))]
        else:
            in_specs += [pl.BlockSpec((tm, tk), lambda i, j, k: (i, k)),
                         pl.BlockSpec((tk, tn), lambda i, j, k: (k, j))]
        args += [x, w]
    for arr, off in extras:
        assert off % tn == 0
        ob = off // tn
        if nk == 1:
            in_specs.append(pl.BlockSpec((tm, tn), order(lambda i, j, ob=ob: (i, j + ob))))
        else:
            in_specs.append(pl.BlockSpec((tm, tn), lambda i, j, k, ob=ob: (i, j + ob)))
        args.append(arr)
    if nk == 1:
        grid = (n_out // tn, m // tm) if cast else (m // tm, n_out // tn)
        out_spec = pl.BlockSpec((tm, tn), order(lambda i, j: (i, j)))
        scratch = [pltpu.VMEM((x.shape[1], tn), BF16) for x, _ in pairs] if cast else []
    else:
        grid = (m // tm, n_out // tn, nk)
        out_spec = pl.BlockSpec((tm, tn), lambda i, j, k: (i, j))
        scratch = [pltpu.VMEM((tm, tn), F32)]
    return pl.pallas_call(
        functools.partial(_matmul_kernel, n_pairs=len(pairs), n_extra=len(extras), epilogue=epilogue, nk=nk,
                          n_cast=len(pairs) if cast else 0),
        grid=grid,
        in_specs=in_specs,
        out_specs=out_spec,
        out_shape=jax.ShapeDtypeStruct((m, n_out), out_dtype),
        scratch_shapes=scratch,
        compiler_params=_params(len(grid)),
        name=name,
    )(*args)


def _ep_plain(accs, extras):
    return accs[0]


def _ep_residual(accs, extras):
    return extras[0] + accs[0]


def _ep_gated_merge(accs, extras):
    return jax.nn.sigmoid(extras[0]) * accs[0] + jax.nn.sigmoid(extras[1]) * accs[1]


def _ep_ple(accs, extras):
    return extras[0] + jax.nn.sigmoid(accs[0]) * accs[1]


def _rope_mix(x, cos, sin):
    return x * cos + pltpu.roll(x, A_ROPE // 2, 1) * sin


def _mla_prep_kernel(cq_ref, ckv_ref, kr_ref, gq_ref, gkv_ref, cos_ref, sin_ref,
                     cqn_ref, ckv_out_ref, ckvb_ref, krope_ref):
    cq = cq_ref[...]
    cqn = cq * lax.rsqrt(jnp.mean(cq * cq, axis=-1, keepdims=True) + EPS) * gq_ref[...]
    cqn_ref[...] = cqn.astype(cqn_ref.dtype)
    ckv = ckv_ref[...]
    ckvn = ckv * lax.rsqrt(jnp.mean(ckv * ckv, axis=-1, keepdims=True) + EPS) * gkv_ref[...]
    ckv_out_ref[...] = ckvn
    ckvb_ref[...] = ckvn.astype(ckvb_ref.dtype)
    krope_ref[...] = _rope_mix(kr_ref[...], cos_ref[...], sin_ref[...])[:, :A_ROPE]


def mla_prep(proj, lay, g_qa, g_kva, cos, sin):
    m = proj.shape[0]
    tm = _tile(m, 512, 8)
    (o_cq, w_cq), (o_ckv, w_ckv), (o_kr, w_kr) = lay["cq"], lay["ckv"], lay["kr2"]
    assert o_cq % w_cq == 0 and o_ckv % w_ckv == 0 and o_kr % w_kr == 0
    return pl.pallas_call(
        _mla_prep_kernel,
        grid=(m // tm,),
        in_specs=[pl.BlockSpec((tm, w_cq), lambda i: (i, o_cq // w_cq)),
                  pl.BlockSpec((tm, w_ckv), lambda i: (i, o_ckv // w_ckv)),
                  pl.BlockSpec((tm, w_kr), lambda i: (i, o_kr // w_kr)),
                  pl.BlockSpec((1, w_cq), lambda i: (0, 0)),
                  pl.BlockSpec((1, w_ckv), lambda i: (0, 0)),
                  pl.BlockSpec((tm, LANES), lambda i: (i, 0)),
                  pl.BlockSpec((tm, LANES), lambda i: (i, 0))],
        out_specs=[pl.BlockSpec((tm, w_cq), lambda i: (i, 0)),
                   pl.BlockSpec((tm, w_ckv), lambda i: (i, 0)),
                   pl.BlockSpec((tm, w_ckv), lambda i: (i, 0)),
                   pl.BlockSpec((tm, A_ROPE), lambda i: (i, 0))],
        out_shape=[jax.ShapeDtypeStruct((m, w_cq), BF16),
                   jax.ShapeDtypeStruct((m, w_ckv), F32),
                   jax.ShapeDtypeStruct((m, w_ckv), BF16),
                   jax.ShapeDtypeStruct((m, A_ROPE), F32)],
        compiler_params=_params(1),
        name="mla_prep",
    )(proj, proj, proj, g_qa.reshape(1, -1), g_kva.reshape(1, -1), cos, sin)


def _q_prep_kernel(x_ref, w_ref, cos_ref, sin_ref, *rest, absorbed, hps):
    x, cos, sin = x_ref[...], cos_ref[...], sin_ref[...]
    for h in range(hps):
        acc = _dot(x, w_ref[h])
        nope = (acc[:, :A_NOPE] * A_SCALE).astype(BF16)
        rot = _rope_mix(acc[:, A_NOPE:], cos, sin)[:, :A_ROPE] * A_SCALE
        if absorbed:
            wuk_ref, o_ref = rest
            o_ref[h, :, :KV_LORA] = _dot(nope, wuk_ref[h]).astype(o_ref.dtype)
            o_ref[h, :, KV_LORA:] = rot.astype(o_ref.dtype)
        else:
            (o_ref,) = rest
            o_ref[h, :, :A_NOPE] = nope
            o_ref[h, :, A_NOPE:] = rot.astype(o_ref.dtype)


def q_prep(cqn, wq, cos, sin, wuk_t=None):
    m, kq = cqn.shape
    tm = _tile(m, ROW_TILE, 8)
    wcols = wq.shape[2]
    hps = PREP_HEADS_PER_STEP
    assert A_HEADS % hps == 0
    absorbed = wuk_t is not None
    width = (KV_LORA if absorbed else A_NOPE) + A_ROPE
    in_specs = [pl.BlockSpec((tm, kq), lambda i, h: (i, 0)),
                pl.BlockSpec((hps, kq, wcols), lambda i, h: (h, 0, 0)),
                pl.BlockSpec((tm, LANES), lambda i, h: (i, 0)),
                pl.BlockSpec((tm, LANES), lambda i, h: (i, 0))]
    args = [cqn, wq, cos, sin]
    if absorbed:
        in_specs.append(pl.BlockSpec((hps, A_NOPE, KV_LORA), lambda i, h: (h, 0, 0)))
        args.append(wuk_t)
    return pl.pallas_call(
        functools.partial(_q_prep_kernel, absorbed=absorbed, hps=hps),
        grid=(m // tm, A_HEADS // hps),
        in_specs=in_specs,
        out_specs=pl.BlockSpec((hps, tm, width), lambda i, h: (h, i, 0)),
        out_shape=jax.ShapeDtypeStruct((A_HEADS, m, width), BF16),
        compiler_params=_params(2),
        name="q_prep",
    )(*args)


def _kv_prep_kernel(x_ref, wk_ref, wvt_ref, kr_ref, k_ref, vt_ref, *, blk, hps):
    x = x_ref[...]
    kr = kr_ref[...].astype(k_ref.dtype)
    for h in range(hps):
        k_ref[h, :, :A_NOPE] = _dot(x, wk_ref[h]).astype(k_ref.dtype)
        k_ref[h, :, A_NOPE:] = kr
        vt = _dot_nt(wvt_ref[h], x).astype(vt_ref.dtype)
        for c in range(x.shape[0] // blk):
            vt_ref[h, c, :A_VDIM, :] = vt[:, c * blk:(c + 1) * blk]
            vt_ref[h, c, A_VDIM:, :] = jnp.ones((A_VDIM, blk), vt_ref.dtype)


def kv_prep(ckv_b, krope, wk, wvt, blk):
    m = ckv_b.shape[0]
    tm = _tile(m, ROW_TILE, blk)
    hps = PREP_HEADS_PER_STEP
    assert A_HEADS % hps == 0
    return pl.pallas_call(
        functools.partial(_kv_prep_kernel, blk=blk, hps=hps),
        grid=(m // tm, A_HEADS // hps),
        in_specs=[pl.BlockSpec((tm, KV_LORA), lambda i, h: (i, 0)),
                  pl.BlockSpec((hps, KV_LORA, A_NOPE), lambda i, h: (h, 0, 0)),
                  pl.BlockSpec((hps, A_VDIM, KV_LORA), lambda i, h: (h, 0, 0)),
                  pl.BlockSpec((tm, A_ROPE), lambda i, h: (i, 0))],
        out_specs=[pl.BlockSpec((hps, tm, A_NOPE + A_ROPE), lambda i, h: (h, i, 0)),
                   pl.BlockSpec((hps, tm // blk, 2 * A_VDIM, blk), lambda i, h: (h, i, 0, 0))],
        out_shape=[jax.ShapeDtypeStruct((A_HEADS, m, A_NOPE + A_ROPE), BF16),
                   jax.ShapeDtypeStruct((A_HEADS, m // blk, 2 * A_VDIM, blk), BF16)],
        compiler_params=_params(2),
        name="kv_prep",
    )(ckv_b, wk, wvt, krope)


def _softmax_step(s, v, m, l, acc):
    m_new = jnp.maximum(m, jnp.max(s, axis=1, keepdims=True))
    p = jnp.exp(s - m_new)
    alpha = jnp.exp(m - m_new)
    l = alpha * l + jnp.sum(p, axis=1, keepdims=True)
    acc = alpha * acc + _dot(p.astype(BF16), v)
    return m_new, l, acc


def _flash_kernel(q_ref, k_ref, vt_ref, o_ref, m_ref, acc_ref, *, blk, hps):
    qi = pl.program_id(2)
    m_ref[...] = jnp.full_like(m_ref, -jnp.inf)
    acc_ref[...] = jnp.zeros_like(acc_ref)

    def block(kj, diagonal):
        start = pl.multiple_of(kj * blk, blk)
        scores = [_dot_nt(k_ref[h, pl.ds(start, blk), :], q_ref[h]) for h in range(hps)]
        if diagonal:
            key = lax.broadcasted_iota(jnp.int32, (blk, blk), 0)
            qry = lax.broadcasted_iota(jnp.int32, (blk, blk), 1)
            scores = [jnp.where(key <= qry, s, -jnp.inf) for s in scores]
        probs, alphas = [], []
        for h, s in enumerate(scores):
            m_old = m_ref[h]
            m_new = jnp.maximum(m_old, jnp.max(s, axis=0, keepdims=True))
            probs.append(jnp.exp(s - m_new).astype(BF16))
            alphas.append(jnp.exp(m_old - m_new))
            m_ref[h] = m_new
        for h in range(hps):
            acc_ref[h] = alphas[h] * acc_ref[h] + _dot(vt_ref[h, kj], probs[h])

    def body(kj, carry):
        block(kj, False)
        return carry

    lax.fori_loop(0, qi, body, 0)
    block(qi, True)
    for h in range(hps):
        acc = acc_ref[h]
        out_t = acc[:A_VDIM] / acc[A_VDIM:A_VDIM + 1]
        o_ref[:, h * A_VDIM:(h + 1) * A_VDIM] = out_t.T.astype(o_ref.dtype)


def flash_prompt(q, k, vt, batch, seq, blk):
    nq = seq // blk
    dqk = q.shape[2]
    hps = ATT_HEADS_PER_STEP
    assert A_HEADS % hps == 0
    return pl.pallas_call(
        functools.partial(_flash_kernel, blk=blk, hps=hps),
        grid=(batch, A_HEADS // hps, nq),
        in_specs=[pl.BlockSpec((hps, blk, dqk), lambda b, h, i: (h, b * nq + i, 0)),
                  pl.BlockSpec((hps, seq, dqk), lambda b, h, i: (h, b, 0)),
                  pl.BlockSpec((hps, nq, 2 * A_VDIM, blk), lambda b, h, i: (h, b, 0, 0))],
        out_specs=pl.BlockSpec((blk, hps * A_VDIM), lambda b, h, i: (b * nq + i, h)),
        out_shape=jax.ShapeDtypeStruct((batch * seq, A_HEADS * A_VDIM), BF16),
        scratch_shapes=[pltpu.VMEM((hps, 1, blk), F32), pltpu.VMEM((hps, 2 * A_VDIM, blk), F32)],
        compiler_params=_params(3),
        name="flash_prompt",
    )(q, k, vt)


def _paged_kernel(pt_ref, q_ref, *refs, n_steps, dec_seq):
    pp = PAGES_PER_STEP
    ckv_refs, kr_refs = refs[:pp], refs[pp:2 * pp]
    ckv_new_ref, kr_new_ref, o_ref, m_ref, l_ref, acc_ref = refs[2 * pp:]
    g = pl.program_id(1)

    @pl.when(g == 0)
    def _():
        m_ref[...] = jnp.full_like(m_ref, -jnp.inf)
        l_ref[...] = jnp.zeros_like(l_ref)
        acc_ref[...] = jnp.zeros_like(acc_ref)

    q = q_ref[...]
    q_lat, q_rope = q[:, :KV_LORA], q[:, KV_LORA:]
    grp = PAGES_PER_GROUP
    kcs, scores = [], []
    for lo in range(0, pp, grp):
        kc = jnp.concatenate([r[...].astype(BF16) for r in ckv_refs[lo:lo + grp]], axis=0)
        krt = jnp.concatenate([r[...].astype(BF16) for r in kr_refs[lo:lo + grp]], axis=1)
        kcs.append(kc)
        scores.append(_dot_nt(q_lat, kc) + _dot(q_rope, krt))
    m_old = m_ref[...]
    m_new = m_old
    for s in scores:
        m_new = jnp.maximum(m_new, jnp.max(s, axis=1, keepdims=True))
    alpha = jnp.exp(m_old - m_new)
    l = alpha * l_ref[...]
    acc = alpha * acc_ref[...]
    for s, kc in zip(scores, kcs):
        p = jnp.exp(s - m_new)
        l = l + jnp.sum(p, axis=1, keepdims=True)
        acc = acc + _dot(p.astype(BF16), kc)
    m_ref[...] = m_new
    l_ref[...] = l
    acc_ref[...] = acc

    @pl.when(g == n_steps - 1)
    def _():
        def pad_rows(x):
            return jnp.concatenate([x, jnp.zeros((PAGE_SIZE - dec_seq, x.shape[1]), x.dtype)], axis=0)

        kcn = pad_rows(ckv_new_ref[...]).astype(BF16)
        krn = pad_rows(kr_new_ref[...]).astype(BF16)
        sn = _dot_nt(q_lat, kcn) + _dot_nt(q_rope, krn)
        tok = lax.broadcasted_iota(jnp.int32, sn.shape, 0) % dec_seq
        key = lax.broadcasted_iota(jnp.int32, sn.shape, 1)
        sn = jnp.where(key <= tok, sn, -jnp.inf)
        _, l2, acc2 = _softmax_step(sn, kcn, m_ref[...], l_ref[...], acc_ref[...])
        o_ref[...] = (acc2 / l2).astype(o_ref.dtype)


def paged_attention(q, pool_ckv, pool_kr, page_table, ckv_new, kr_new):
    nb, rows, dq = q.shape
    n_pages = page_table.shape[1]
    dec_seq = ckv_new.shape[1]
    pp = PAGES_PER_STEP
    assert n_pages % pp == 0 and pp % PAGES_PER_GROUP == 0
    n_steps = n_pages // pp
    page = pool_ckv.shape[1]
    assert page == PAGE_SIZE and pool_kr.shape[1:] == (A_ROPE, page)

    def page_map(i):
        return lambda b, g, pt: (pt[b * n_pages + g * pp + i], 0, 0)

    in_specs = [pl.BlockSpec((None, rows, dq), lambda b, g, pt: (b, 0, 0))]
    in_specs += [pl.BlockSpec((None, page, KV_LORA), page_map(i)) for i in range(pp)]
    in_specs += [pl.BlockSpec((None, A_ROPE, page), page_map(i)) for i in range(pp)]
    in_specs += [pl.BlockSpec((None, dec_seq, KV_LORA), lambda b, g, pt: (b, 0, 0)),
                 pl.BlockSpec((None, dec_seq, A_ROPE), lambda b, g, pt: (b, 0, 0))]
    grid_spec = pltpu.PrefetchScalarGridSpec(
        num_scalar_prefetch=1,
        grid=(nb, n_steps),
        in_specs=in_specs,
        out_specs=pl.BlockSpec((None, rows, KV_LORA), lambda b, g, pt: (b, 0, 0)),
        scratch_shapes=[pltpu.VMEM((rows, 1), F32), pltpu.VMEM((rows, 1), F32),
                        pltpu.VMEM((rows, KV_LORA), F32)],
    )
    return pl.pallas_call(
        functools.partial(_paged_kernel, n_steps=n_steps, dec_seq=dec_seq),
        grid_spec=grid_spec,
        out_shape=jax.ShapeDtypeStruct((nb, rows, KV_LORA), BF16),
        compiler_params=_params(2),
        name="paged_attention",
    )(page_table.reshape(-1), q, *([pool_ckv] * pp), *([pool_kr] * pp), ckv_new, kr_new)


def _head_mm_kernel(x_ref, w_ref, o_ref):
    o_ref[...] = _dot(x_ref[...], w_ref[...]).astype(o_ref.dtype)


def head_matmul(x, w):
    nh, m, kd = x.shape
    n = w.shape[2]
    return pl.pallas_call(
        _head_mm_kernel,
        grid=(nh,),
        in_specs=[pl.BlockSpec((None, m, kd), lambda h: (h, 0, 0)),
                  pl.BlockSpec((None, kd, n), lambda h: (h, 0, 0))],
        out_specs=pl.BlockSpec((m, n), lambda h: (0, h)),
        out_shape=jax.ShapeDtypeStruct((m, nh * n), BF16),
        compiler_params=_params(1),
        name="head_matmul",
    )(x, w)


def _mlstm_kernel(q_ref, k_ref, v_ref, if_ref, mo_ref, bias_ref, gmh_ref, c0_ref, n0_ref, m0_ref,
                  o_ref, c_ref, n_ref, m_ref, *, rows, lp):
    chunk = pl.program_id(1)

    @pl.when(chunk == 0)
    def _():
        c_ref[...] = c0_ref[...]
        n_ref[...] = n0_ref[...]
        m_ref[...] = m0_ref[...]

    def pad(x):
        if rows == lp:
            return x
        return jnp.concatenate([x, jnp.zeros((lp - rows, x.shape[1]), x.dtype)], axis=0)

    t_idx = lax.broadcasted_iota(jnp.int32, (lp, lp), 0)
    s_idx = lax.broadcasted_iota(jnp.int32, (lp, lp), 1)
    causal = s_idx <= t_idx
    diag = s_idx == t_idx
    valid = lax.broadcasted_iota(jnp.int32, (lp, 1), 0) < rows
    gates = pad(if_ref[...] + bias_ref[...])

    for h in range(M_HEADS):
        q = pad(q_ref[:, h * M_DQK:(h + 1) * M_DQK]) * (M_DQK ** -0.5)
        k = pad(k_ref[:, h * M_DQK:(h + 1) * M_DQK])
        v = pad(v_ref[:, h * M_DV:(h + 1) * M_DV])
        i_col = gates[:, h:h + 1]
        f_pre = gates[:, M_HEADS + h:M_HEADS + h + 1]
        f_col = jnp.minimum(f_pre, 0.0) - jnp.log1p(jnp.exp(-jnp.abs(f_pre)))
        if rows != lp:
            i_col = jnp.where(valid, i_col, -jnp.inf)
            f_col = jnp.where(valid, f_col, 0.0)

        f_row = jnp.sum(jnp.where(diag, f_col, 0.0), axis=0, keepdims=True)
        i_row = jnp.sum(jnp.where(diag, i_col, 0.0), axis=0, keepdims=True)
        b_col = jnp.sum(jnp.where(causal, f_row, 0.0), axis=1, keepdims=True)
        b_row = jnp.sum(jnp.where(t_idx <= s_idx, f_col, 0.0), axis=0, keepdims=True)

        m_prev = m_ref[h]
        c_prev = c_ref[h]
        n_prev = n_ref[h]
        a_col = b_col + m_prev
        dmat = jnp.where(causal, b_col - b_row + i_row, -jnp.inf)
        mt = jnp.maximum(a_col, jnp.max(dmat, axis=1, keepdims=True))
        w_inter = jnp.exp(a_col - mt)
        qb, kb, vb = q.astype(BF16), k.astype(BF16), v.astype(BF16)
        qk = _dot_nt(qb, kb) * jnp.exp(dmat - mt)
        num = _dot(qk.astype(BF16), vb) + w_inter * _dot(qb, c_prev.astype(BF16))
        den = jnp.sum(qk, axis=1, keepdims=True) + w_inter * jnp.sum(q * n_prev, axis=1, keepdims=True)
        hid = num / jnp.maximum(jnp.abs(den), jnp.exp(-mt))

        b_last = b_col[lp - 1:lp, :]
        g_col = b_last - b_col + i_col
        m_new = jnp.maximum(b_last + m_prev, jnp.max(g_col, axis=0, keepdims=True))
        decay = jnp.exp(b_last + m_prev - m_new)
        kw = jnp.exp(g_col - m_new) * k
        c_ref[h] = decay * c_prev + _dot_tn(kw.astype(BF16), vb)
        n_ref[h] = decay * n_prev + jnp.sum(kw, axis=0, keepdims=True)
        m_ref[h] = m_new

        hn = hid * lax.rsqrt(jnp.mean(hid * hid, axis=1, keepdims=True) + EPS) * gmh_ref[h]
        out = jax.nn.sigmoid(pad(mo_ref[:, h * M_DV:(h + 1) * M_DV])) * hn
        o_ref[:, h * M_DV:(h + 1) * M_DV] = out[:rows].astype(o_ref.dtype)


def mlstm(proj, lay, bias_row, g_mh, c0, n0, m0, batch, seq):
    rows = _tile(seq, MLSTM_CHUNK, 8)
    lp = max(rows, MLSTM_MIN_ROWS)
    nc = seq // rows
    hq, hv = M_HEADS * M_DQK, M_HEADS * M_DV
    oq, ok, ov, oi, oo = (lay[n][0] for n in ("mq", "mk", "mv", "mif", "mo"))
    assert oq % hq == 0 and ok % hq == 0 and ov % hv == 0 and oo % hv == 0 and oi % LANES == 0

    def rowmap(col_block):
        return lambda b, c: (b * nc + c, col_block)

    def state_spec(*dims):
        return pl.BlockSpec((None, M_HEADS) + dims, lambda b, c: (b, 0, 0, 0))

    out, c1, n1, m1 = pl.pallas_call(
        functools.partial(_mlstm_kernel, rows=rows, lp=lp),
        grid=(batch, nc),
        in_specs=[pl.BlockSpec((rows, hq), rowmap(oq // hq)),
                  pl.BlockSpec((rows, hq), rowmap(ok // hq)),
                  pl.BlockSpec((rows, hv), rowmap(ov // hv)),
                  pl.BlockSpec((rows, LANES), rowmap(oi // LANES)),
                  pl.BlockSpec((rows, hv), rowmap(oo // hv)),
                  pl.BlockSpec((1, LANES), lambda b, c: (0, 0)),
                  pl.BlockSpec((M_HEADS, 1, M_DV), lambda b, c: (0, 0, 0)),
                  state_spec(M_DQK, M_DV), state_spec(1, M_DQK), state_spec(1, 1)],
        out_specs=[pl.BlockSpec((rows, hv), lambda b, c: (b * nc + c, 0)),
                   state_spec(M_DQK, M_DV), state_spec(1, M_DQK), state_spec(1, 1)],
        out_shape=[jax.ShapeDtypeStruct((batch * seq, hv), BF16),
                   jax.ShapeDtypeStruct((batch, M_HEADS, M_DQK, M_DV), F32),
                   jax.ShapeDtypeStruct((batch, M_HEADS, 1, M_DQK), F32),
                   jax.ShapeDtypeStruct((batch, M_HEADS, 1, 1), F32)],
        compiler_params=_params(2),
        name="mlstm",
    )(proj, proj, proj, proj, proj, bias_row, g_mh.reshape(M_HEADS, 1, M_DV),
      c0, n0.reshape(batch, M_HEADS, 1, M_DQK), m0.reshape(batch, M_HEADS, 1, 1))
    return out, c1, n1.reshape(batch, M_HEADS, M_DQK), m1.reshape(batch, M_HEADS)


def _ffn_up_kernel(x_ref, wg_ref, wv_ref, c0_ref, w_ref, b_ref, act_ref, tail_ref, wg_bf, wv_bf, *carry,
                   nb, tiles_per_seq):
    i = pl.program_id(1)
    _cast_weights_once([wg_ref, wv_ref], [wg_bf, wv_bf])
    x = x_ref[...]
    tm, tc = act_ref.shape
    g = _dot(x, wg_bf[...]).reshape(nb, tm // nb, tc)
    val = _dot(x, wv_bf[...]).reshape(nb, tm // nb, tc)
    st = tm // nb
    last_rows = g[:, st - (CONV_W - 1):, :]
    prev = c0_ref[...]
    if tiles_per_seq > 1:
        carry_ref = carry[0]

        @pl.when(i == 0)
        def _():
            carry_ref[...] = jnp.zeros_like(carry_ref)

        prev = jnp.where(i % tiles_per_seq == 0, prev, carry_ref[...])
        carry_ref[...] = last_rows
    t = lax.broadcasted_iota(jnp.int32, g.shape, 1)
    prev1 = jnp.where(t == 0, prev[:, 1:2, :], pltpu.roll(g, 1, 1))
    prev2 = jnp.where(t == 0, prev[:, 0:1, :], jnp.where(t == 1, prev[:, 1:2, :], pltpu.roll(g, 2, 1)))
    w = w_ref[...]
    conv = b_ref[...] + w[0:1, :] * prev2 + w[1:2, :] * prev1 + w[2:3, :] * g
    act_ref[...] = (jax.nn.gelu(conv, approximate=True) * val).reshape(tm, tc).astype(act_ref.dtype)
    tail_ref[...] = last_rows


def ffn_up_convglu(x, w_up, conv0, w_conv, b_conv, batch, seq):
    assert CONV_W == 3 and seq >= CONV_W - 1
    m, kd = x.shape
    tc = _tile(D_FF, 256)
    ncol = D_FF // tc
    if seq >= ROW_TILE:
        tm, nb = _tile(seq, ROW_TILE, 8), 1
    else:
        nb = max(1, min(batch, ROW_TILE // seq))
        assert batch % nb == 0
        tm = nb * seq
    tps = seq // (tm // nb)
    assert w_up.dtype == F32
    scratch = [pltpu.VMEM((kd, tc), BF16), pltpu.VMEM((kd, tc), BF16)]
    if tps > 1:
        scratch.append(pltpu.VMEM((nb, CONV_W - 1, tc), F32))
    act, tail = pl.pallas_call(
        functools.partial(_ffn_up_kernel, nb=nb, tiles_per_seq=tps),
        grid=(ncol, m // tm),
        in_specs=[pl.BlockSpec((tm, kd), lambda j, i: (i, 0)),
                  pl.BlockSpec((kd, tc), lambda j, i: (0, j)),
                  pl.BlockSpec((kd, tc), lambda j, i: (0, j + ncol)),
                  pl.BlockSpec((nb, CONV_W - 1, tc), lambda j, i: (i // tps, 0, j)),
                  pl.BlockSpec((CONV_W, tc), lambda j, i: (0, j)),
                  pl.BlockSpec((1, tc), lambda j, i: (0, j))],
        out_specs=[pl.BlockSpec((tm, tc), lambda j, i: (i, j)),
                   pl.BlockSpec((nb, CONV_W - 1, tc), lambda j, i: (i, 0, j))],
        out_shape=[jax.ShapeDtypeStruct((m, D_FF), BF16),
                   jax.ShapeDtypeStruct((batch * tps, CONV_W - 1, D_FF), F32)],
        scratch_shapes=scratch,
        compiler_params=_params(2),
        name="ffn_up_convglu",
    )(x, w_up, w_up, conv0, w_conv, b_conv.reshape(1, D_FF))
    return act, tail.reshape(batch, tps, CONV_W - 1, D_FF)[:, tps - 1]


_MODE_COPY, _MODE_DUP_ROPE, _MODE_GATES, _MODE_ZERO = 0, 1, 2, 3


def _relayout_kernel(off_ref, mode_ref, src_ref, o_ref):
    mode = mode_ref[pl.program_id(0)]
    row = lax.broadcasted_iota(jnp.int32, src_ref.shape, 0)

    @pl.when(mode == _MODE_COPY)
    def _():
        o_ref[...] = src_ref[...].T.astype(o_ref.dtype)

    @pl.when(mode == _MODE_DUP_ROPE)
    def _():
        kr = src_ref[:A_ROPE, :]
        o_ref[...] = jnp.concatenate([kr, kr], axis=0).T.astype(o_ref.dtype)

    @pl.when(mode == _MODE_GATES)
    def _():
        o_ref[...] = jnp.where(row < 2 * M_HEADS, src_ref[...], 0.0).T.astype(o_ref.dtype)

    @pl.when(mode == _MODE_ZERO)
    def _():
        o_ref[...] = jnp.zeros_like(o_ref)


def relayout_w_in(w_in_t, lay, total, src_off):
    assert 2 * A_ROPE == LANES and 2 * M_HEADS <= LANES
    n_tiles = total // LANES
    off, mode = np.zeros(n_tiles, np.int32), np.full(n_tiles, _MODE_ZERO, np.int32)
    groups = [(lay[n][0], lay[n][1], src_off[n], _MODE_COPY) for n in ("cq", "mq", "mk", "ckv", "mv", "mo", "ga", "gb")]
    groups += [(*lay["kr2"], src_off["kr"], _MODE_DUP_ROPE), (*lay["mif"], src_off["mi"], _MODE_GATES)]
    n_src, rows = w_in_t.shape
    for dst, width, s0, md in groups:
        assert dst % LANES == 0 and width % LANES == 0 and s0 % 8 == 0
        for t in range(width // LANES):
            off[dst // LANES + t], mode[dst // LANES + t] = s0 + t * LANES, md
    assert int(off.max()) + LANES <= n_src
    return pl.pallas_call(
        _relayout_kernel,
        grid_spec=pltpu.PrefetchScalarGridSpec(
            num_scalar_prefetch=2,
            grid=(n_tiles,),
            in_specs=[pl.BlockSpec((pl.Element(LANES), pl.Element(rows)),
                                   lambda t, off8, mode: (pl.multiple_of(off8[t] * 8, 8), 0))],
            out_specs=pl.BlockSpec((rows, LANES), lambda t, off, mode: (0, t)),
        ),
        out_shape=jax.ShapeDtypeStruct((rows, total), BF16),
        compiler_params=_params(1),
        name="relayout_w_in",
    )(jnp.asarray(off // 8), jnp.asarray(mode), w_in_t)


def _prepare_weights(w):
    lay, total = _padded_layout()
    offs = np.concatenate([[0], np.cumsum([Q_LORA, KV_LORA, A_ROPE, M_HEADS * M_DQK, M_HEADS * M_DQK,
                                            M_HEADS * M_DV, M_HEADS, M_HEADS, M_HEADS * M_DV, D_MODEL, D_MODEL])])
    src = {n: (int(offs[i]), int(offs[i + 1])) for i, n in enumerate(
        ["cq", "ckv", "kr", "mq", "mk", "mv", "mi", "mf", "mo", "ga", "gb"])}
    assert src["mf"][0] == src["mi"][1]
    w_in_p = relayout_w_in(jnp.swapaxes(w["w_in"], 0, 1), lay, total, {n: s[0] for n, s in src.items()})

    w_uq = w["w_uq"]
    rope_cols = w_uq[..., A_NOPE:]
    wq = jnp.concatenate([w_uq[..., :A_NOPE], rope_cols, rope_cols], axis=-1).transpose(1, 0, 2).astype(BF16)
    bias_row = jnp.concatenate([w["b_i"].astype(F32), w["b_f"].astype(F32),
                                jnp.zeros((LANES - 2 * M_HEADS,), F32)]).reshape(1, LANES)
    return {
        "lay": lay, "n_proj": total, "w_in_p": w_in_p, "wq": wq,
        "wuk_t": w["w_uk"].transpose(1, 2, 0).astype(BF16),
        "wk": w["w_uk"].transpose(1, 0, 2).astype(BF16),
        "wv": w["w_uv"].transpose(1, 0, 2).astype(BF16),
        "wvt": w["w_uv"].transpose(1, 2, 0).astype(BF16),
        "bias_row": bias_row,
        "w_br_a": w["w_br_a"].astype(F32), "w_br_b": w["w_br_b"].astype(F32),
        "w_o": w["w_o"].astype(F32), "w_up": w["w_up"].astype(F32), "w_down": w["w_down"].astype(BF16),
        "w_ple_gate": w["w_ple_gate"].astype(F32), "w_ple_proj": w["w_ple_proj"].astype(F32),
    }


def _rope_tables(pos, batch):
    half = A_ROPE // 2
    inv = jnp.power(ROPE_THETA, -jnp.arange(half, dtype=F32) / half)
    ang = pos[:, None] * inv[None, :]
    cos, sin = jnp.cos(ang), jnp.sin(ang)
    reps = LANES // A_ROPE
    cos_t = jnp.tile(jnp.concatenate([cos, cos], axis=1), (batch, reps))
    sin_t = jnp.tile(jnp.concatenate([-sin, sin], axis=1), (batch, reps))
    return cos_t, sin_t


def _hybrid_layer(x, pe, pos, w, pw, c0, n0, m0, conv0, paged):
    batch, seq, _ = x.shape
    m = batch * seq
    lay = pw["lay"]
    x2 = x.reshape(m, D_MODEL)
    cos, sin = _rope_tables(pos, batch)

    h = rmsnorm_rows(x2, w["g_mix"], BF16)
    proj = fused_matmul([(h, pw["w_in_p"])], [], _ep_plain, pw["n_proj"], F32,
                        tn=_tile(pw["n_proj"], 1024), name="in_proj")
    cqn, c_kv, ckv_b, k_rope = mla_prep(proj, lay, w["g_qa"], w["g_kva"], cos, sin)

    if paged is None:
        q = q_prep(cqn, pw["wq"], cos, sin)
        blk = _tile(seq, ATT_BLOCK)
        k, vt = kv_prep(ckv_b, k_rope, pw["wk"], pw["wvt"], blk)
        a_out = flash_prompt(q, k, vt, batch, seq, blk)
    else:
        pool_ckv, pool_kr, page_table = paged
        q = q_prep(cqn, pw["wq"], cos, sin, pw["wuk_t"])
        dq = q.shape[2]
        q = q.reshape(A_HEADS, batch, seq, dq).transpose(1, 0, 2, 3).reshape(batch, A_HEADS * seq, dq)
        o_lat = paged_attention(q, pool_ckv, pool_kr, page_table,
                                c_kv.reshape(batch, seq, KV_LORA), k_rope.reshape(batch, seq, A_ROPE))
        o_lat = o_lat.reshape(batch, A_HEADS, seq, KV_LORA).transpose(1, 0, 2, 3).reshape(A_HEADS, m, KV_LORA)
        a_out = head_matmul(o_lat, pw["wv"])

    b_out, c1, n1, m1 = mlstm(proj, lay, pw["bias_row"], w["g_mh"], c0, n0, m0, batch, seq)

    merged = fused_matmul([(a_out, pw["w_br_a"]), (b_out, pw["w_br_b"])],
                          [(proj, lay["ga"][0]), (proj, lay["gb"][0])],
                          _ep_gated_merge, D_MODEL, BF16, name="branch_merge")
    x2 = fused_matmul([(merged, pw["w_o"])], [(x2, 0)], _ep_residual, D_MODEL, F32, name="out_proj")

    hf = rmsnorm_rows(x2, w["g_ffn"], BF16)
    act, conv_tail = ffn_up_convglu(hf, pw["w_up"], conv0.astype(F32), w["w_conv"], w["b_conv"], batch, seq)
    x2 = fused_matmul([(act, pw["w_down"])], [(x2, 0)], _ep_residual, D_MODEL, F32,
                      tn=_tile(D_MODEL, 512), tk=_tile(D_FF, D_FF // 2), name="ffn_down")

    hp = rmsnorm_rows(x2, w["g_ple"], BF16)
    x2 = fused_matmul([(hp, pw["w_ple_gate"]), (pe.reshape(m, PLE_DIM).astype(BF16), pw["w_ple_proj"])],
                      [(x2, 0)], _ep_ple, D_MODEL, F32, name="ple")
    return (x2, c_kv.reshape(batch, seq, KV_LORA), k_rope.reshape(batch, seq, A_ROPE), c1, n1, m1, conv_tail)


def kernel(x_prompt, x_sample, cache_ckv, cache_krope, state_C, state_n, state_m, state_conv, page_table,
           p_prompt, p_sample, g_mix, w_in, b_i, b_f, g_qa, w_uq, g_kva, w_uk, w_uv, g_mh, w_br_a, w_br_b,
           w_o, g_ffn, w_up, w_conv, b_conv, w_down, g_ple, w_ple_gate, w_ple_proj, g_final):
    bp, sp = x_prompt.shape[0], x_prompt.shape[1]
    bs, ss = x_sample.shape[0], x_sample.shape[1]
    depth = w_in.shape[0]
    past = page_table.shape[1] * cache_ckv.shape[2]
    pos_p = jnp.arange(sp, dtype=F32)
    pos_s = past + jnp.arange(ss, dtype=F32)
    c0p = jnp.zeros((bp, M_HEADS, M_DQK, M_DV), F32)
    n0p = jnp.zeros((bp, M_HEADS, M_DQK), F32)
    m0p = jnp.zeros((bp, M_HEADS), F32)
    conv0p = jnp.zeros((bp, CONV_W - 1, D_FF), F32)
    names = ["g_mix", "w_in", "b_i", "b_f", "g_qa", "w_uq", "g_kva", "w_uk", "w_uv", "g_mh", "w_br_a", "w_br_b",
             "w_o", "g_ffn", "w_up", "w_conv", "b_conv", "w_down", "g_ple", "w_ple_gate", "w_ple_proj"]
    stacked = [g_mix, w_in, b_i, b_f, g_qa, w_uq, g_kva, w_uk, w_uv, g_mh, w_br_a, w_br_b,
               w_o, g_ffn, w_up, w_conv, b_conv, w_down, g_ple, w_ple_gate, w_ple_proj]
    xp, xs = x_prompt, x_sample
    outs_p = [[] for _ in range(6)]
    outs_s = [[] for _ in range(6)]
    for l in range(depth):
        w = {n: a[l] for n, a in zip(names, stacked)}
        pw = _prepare_weights(w)
        xp2, *new_p = _hybrid_layer(xp, p_prompt[l], pos_p, w, pw, c0p, n0p, m0p, conv0p, None)
        xs2, *new_s = _hybrid_layer(xs, p_sample[l], pos_s, w, pw, state_C[l], state_n[l], state_m[l],
                                    state_conv[l],
                                    (cache_ckv[l], jnp.swapaxes(cache_krope[l], 1, 2), page_table))
        xp, xs = xp2.reshape(xp.shape), xs2.reshape(xs.shape)
        for j in range(6):
            outs_p[j].append(new_p[j])
            outs_s[j].append(new_s[j])
    stack_p = [jnp.stack(o) for o in outs_p]
    stack_s = [jnp.stack(o) for o in outs_s]
    y_prompt = rmsnorm_rows(xp.reshape(bp * sp, D_MODEL), g_final, F32).reshape(xp.shape)
    y_sample = rmsnorm_rows(xs.reshape(bs * ss, D_MODEL), g_final, F32).reshape(xs.shape)
    return (y_prompt, y_sample, *stack_p, *stack_s)
```

```python
import functools

import jax
import jax.numpy as jnp
import numpy as np
from jax import lax
from jax.experimental import pallas as pl
from jax.experimental.pallas import tpu as pltpu

D_MODEL = 4096
DEPTH = 1
PAGE_SIZE = 128
A_HEADS = 16
A_NOPE = 128
A_ROPE = 64
A_VDIM = 128
Q_LORA = 1024
KV_LORA = 512
ROPE_THETA = 10000.0
A_SCALE = (A_NOPE + A_ROPE) ** -0.5
M_HEADS = 4
M_DQK = 256
M_DV = 512
D_FF = 11008
CONV_W = 3
PLE_DIM = 256
EPS = 1e-6

LANES = 128
VMEM_LIMIT = 56 * 1024 * 1024
ROW_TILE = 1024
COL_TILE = 512
MLSTM_CHUNK = 256
MLSTM_MIN_ROWS = 128
MLSTM_SEQS_PER_STEP = 2
ATT_BLOCK = 256
ATT_HEADS_PER_STEP = 8
PREP_HEADS_PER_STEP = 4
PAGES_PER_STEP = 64
PAGES_PER_GROUP = 16

F32 = jnp.float32
BF16 = jnp.bfloat16


def _params(n_grid):
    return pltpu.CompilerParams(dimension_semantics=("arbitrary",) * n_grid,
                                vmem_limit_bytes=VMEM_LIMIT)


def _tile(n, pref, unit=LANES):
    if n <= pref:
        return n
    best = None
    for t in range(unit, pref + 1, unit):
        if n % t == 0:
            best = t
    assert best is not None, (n, pref)
    return best


def _padded_layout():
    hq, hv = M_HEADS * M_DQK, M_HEADS * M_DV
    widths = [("cq", Q_LORA), ("mq", hq), ("mk", hq), ("ckv", KV_LORA), ("kr2", 2 * A_ROPE), ("mif", LANES)]
    off, lay = 0, {}
    for name, w in widths:
        lay[name] = (off, w)
        off += w
    off = -(-off // hv) * hv
    for name, w in [("mv", hv), ("mo", hv), ("ga", D_MODEL), ("gb", D_MODEL)]:
        lay[name] = (off, w)
        off += w
    return lay, off


def _dot(a, b):
    return jnp.dot(a, b, preferred_element_type=F32)


def _dot_nt(a, b):
    return lax.dot_general(a, b, (((1,), (1,)), ((), ())), preferred_element_type=F32)


def _dot_tn(a, b):
    return lax.dot_general(a, b, (((0,), (0,)), ((), ())), preferred_element_type=F32)


def _rmsnorm_kernel(x_ref, g_ref, o_ref):
    x = x_ref[...]
    y = x * lax.rsqrt(jnp.mean(x * x, axis=-1, keepdims=True) + EPS)
    o_ref[...] = (y * g_ref[...]).astype(o_ref.dtype)


def rmsnorm_rows(x, g, out_dtype):
    m, d = x.shape
    tm = _tile(m, 512, 8)
    return pl.pallas_call(
        _rmsnorm_kernel,
        grid=(m // tm,),
        in_specs=[pl.BlockSpec((tm, d), lambda i: (i, 0)),
                  pl.BlockSpec((1, d), lambda i: (0, 0))],
        out_specs=pl.BlockSpec((tm, d), lambda i: (i, 0)),
        out_shape=jax.ShapeDtypeStruct((m, d), out_dtype),
        compiler_params=_params(1),
        name="rmsnorm",
    )(x, g.reshape(1, d).astype(F32))


def _cast_weights_once(w_refs, cast_refs):
    @pl.when(pl.program_id(1) == 0)
    def _():
        for w, c in zip(w_refs, cast_refs):
            c[...] = w[...].astype(c.dtype)


def _matmul_kernel(*refs, n_pairs, n_extra, epilogue, nk, n_cast):
    pairs = [(refs[2 * p], refs[2 * p + 1]) for p in range(n_pairs)]
    extras = refs[2 * n_pairs:2 * n_pairs + n_extra]
    o_ref = refs[2 * n_pairs + n_extra]
    if nk == 1:
        if n_cast:
            cast_refs = refs[2 * n_pairs + n_extra + 1:]
            _cast_weights_once([w for _, w in pairs], cast_refs)
            pairs = [(x, c) for (x, _), c in zip(pairs, cast_refs)]
        accs = [_dot(x[...], w[...]) for x, w in pairs]
        o_ref[...] = epilogue(accs, [e[...] for e in extras]).astype(o_ref.dtype)
        return
    acc_ref = refs[2 * n_pairs + n_extra + 1]
    k = pl.program_id(2)
    x, w = pairs[0]

    @pl.when(k == 0)
    def _():
        acc_ref[...] = jnp.zeros_like(acc_ref)

    acc_ref[...] += _dot(x[...], w[...])

    @pl.when(k == nk - 1)
    def _():
        o_ref[...] = epilogue([acc_ref[...]], [e[...] for e in extras]).astype(o_ref.dtype)


def fused_matmul(pairs, extras, epilogue, n_out, out_dtype, *, tm=None, tn=None, tk=None, name="matmul"):
    m = pairs[0][0].shape[0]
    tm = tm or _tile(m, ROW_TILE, 8)
    tn = tn or _tile(n_out, COL_TILE)
    kdim = pairs[0][0].shape[1]
    nk = 1 if tk is None else kdim // tk
    assert nk == 1 or len(pairs) == 1
    cast = any(w.dtype != BF16 for _, w in pairs)
    assert not cast or (nk == 1 and all(w.dtype == F32 for _, w in pairs))

    def order(f):
        return (lambda j, i: f(i, j)) if cast else f

    in_specs, args = [], []
    for x, w in pairs:
        kd = x.shape[1]
        if nk == 1:
            in_specs += [pl.BlockSpec((tm, kd), order(lambda i, j: (i, 0))),
                         pl.BlockSpec((kd, tn), order(lambda i, j: (0, j)))]
        else:
            in_specs += [pl.BlockSpec((tm, tk), lambda i, j, k: (i, k)),
                         pl.BlockSpec((tk, tn), lambda i, j, k: (k, j))]
        args += [x, w]
    for arr, off in extras:
        assert off % tn == 0
        ob = off // tn
        if nk == 1:
            in_specs.append(pl.BlockSpec((tm, tn), order(lambda i, j, ob=ob: (i, j + ob))))
        else:
            in_specs.append(pl.BlockSpec((tm, tn), lambda i, j, k, ob=ob: (i, j + ob)))
        args.append(arr)
    if nk == 1:
        grid = (n_out // tn, m // tm) if cast else (m // tm, n_out // tn)
        out_spec = pl.BlockSpec((tm, tn), order(lambda i, j: (i, j)))
        scratch = [pltpu.VMEM((x.shape[1], tn), BF16) for x, _ in pairs] if cast else []
    else:
        grid = (m // tm, n_out // tn, nk)
        out_spec = pl.BlockSpec((tm, tn), lambda i, j, k: (i, j))
        scratch = [pltpu.VMEM((tm, tn), F32)]
    return pl.pallas_call(
        functools.partial(_matmul_kernel, n_pairs=len(pairs), n_extra=len(extras), epilogue=epilogue, nk=nk,
                          n_cast=len(pairs) if cast else 0),
        grid=grid,
        in_specs=in_specs,
        out_specs=out_spec,
        out_shape=jax.ShapeDtypeStruct((m, n_out), out_dtype),
        scratch_shapes=scratch,
        compiler_params=_params(len(grid)),
        name=name,
    )(*args)


def _ep_plain(accs, extras):
    return accs[0]


def _ep_residual(accs, extras):
    return extras[0] + accs[0]


def _ep_gated_merge(accs, extras):
    return jax.nn.sigmoid(extras[0]) * accs[0] + jax.nn.sigmoid(extras[1]) * accs[1]


def _ep_ple(accs, extras):
    return extras[0] + jax.nn.sigmoid(accs[0]) * accs[1]


def _rope_mix(x, cos, sin):
    return x * cos + pltpu.roll(x, A_ROPE // 2, 1) * sin


def _mla_prep_kernel(cq_ref, ckv_ref, kr_ref, gq_ref, gkv_ref, cos_ref, sin_ref,
                     cqn_ref, ckv_out_ref, ckvb_ref, krope_ref):
    cq = cq_ref[...]
    cqn = cq * lax.rsqrt(jnp.mean(cq * cq, axis=-1, keepdims=True) + EPS) * gq_ref[...]
    cqn_ref[...] = cqn.astype(cqn_ref.dtype)
    ckv = ckv_ref[...]
    ckvn = ckv * lax.rsqrt(jnp.mean(ckv * ckv, axis=-1, keepdims=True) + EPS) * gkv_ref[...]
    ckv_out_ref[...] = ckvn
    ckvb_ref[...] = ckvn.astype(ckvb_ref.dtype)
    krope_ref[...] = _rope_mix(kr_ref[...], cos_ref[...], sin_ref[...])[:, :A_ROPE]


def mla_prep(proj, lay, g_qa, g_kva, cos, sin):
    m = proj.shape[0]
    tm = _tile(m, 512, 8)
    (o_cq, w_cq), (o_ckv, w_ckv), (o_kr, w_kr) = lay["cq"], lay["ckv"], lay["kr2"]
    assert o_cq % w_cq == 0 and o_ckv % w_ckv == 0 and o_kr % w_kr == 0
    return pl.pallas_call(
        _mla_prep_kernel,
        grid=(m // tm,),
        in_specs=[pl.BlockSpec((tm, w_cq), lambda i: (i, o_cq // w_cq)),
                  pl.BlockSpec((tm, w_ckv), lambda i: (i, o_ckv // w_ckv)),
                  pl.BlockSpec((tm, w_kr), lambda i: (i, o_kr // w_kr)),
                  pl.BlockSpec((1, w_cq), lambda i: (0, 0)),
                  pl.BlockSpec((1, w_ckv), lambda i: (0, 0)),
                  pl.BlockSpec((tm, LANES), lambda i: (i, 0)),
                  pl.BlockSpec((tm, LANES), lambda i: (i, 0))],
        out_specs=[pl.BlockSpec((tm, w_cq), lambda i: (i, 0)),
                   pl.BlockSpec((tm, w_ckv), lambda i: (i, 0)),
                   pl.BlockSpec((tm, w_ckv), lambda i: (i, 0)),
                   pl.BlockSpec((tm, A_ROPE), lambda i: (i, 0))],
        out_shape=[jax.ShapeDtypeStruct((m, w_cq), BF16),
                   jax.ShapeDtypeStruct((m, w_ckv), F32),
                   jax.ShapeDtypeStruct((m, w_ckv), BF16),
                   jax.ShapeDtypeStruct((m, A_ROPE), F32)],
        compiler_params=_params(1),
        name="mla_prep",
    )(proj, proj, proj, g_qa.reshape(1, -1), g_kva.reshape(1, -1), cos, sin)


def _q_prep_kernel(x_ref, w_ref, cos_ref, sin_ref, *rest, absorbed, hps):
    x, cos, sin = x_ref[...], cos_ref[...], sin_ref[...]
    for h in range(hps):
        acc = _dot(x, w_ref[h])
        nope = (acc[:, :A_NOPE] * A_SCALE).astype(BF16)
        rot = _rope_mix(acc[:, A_NOPE:], cos, sin)[:, :A_ROPE] * A_SCALE
        if absorbed:
            wuk_ref, o_ref = rest
            o_ref[h, :, :KV_LORA] = _dot(nope, wuk_ref[h]).astype(o_ref.dtype)
            o_ref[h, :, KV_LORA:] = rot.astype(o_ref.dtype)
        else:
            (o_ref,) = rest
            o_ref[h, :, :A_NOPE] = nope
            o_ref[h, :, A_NOPE:] = rot.astype(o_ref.dtype)


def q_prep(cqn, wq, cos, sin, wuk_t=None):
    m, kq = cqn.shape
    tm = _tile(m, ROW_TILE, 8)
    wcols = wq.shape[2]
    hps = PREP_HEADS_PER_STEP
    assert A_HEADS % hps == 0
    absorbed = wuk_t is not None
    width = (KV_LORA if absorbed else A_NOPE) + A_ROPE
    in_specs = [pl.BlockSpec((tm, kq), lambda i, h: (i, 0)),
                pl.BlockSpec((hps, kq, wcols), lambda i, h: (h, 0, 0)),
                pl.BlockSpec((tm, LANES), lambda i, h: (i, 0)),
                pl.BlockSpec((tm, LANES), lambda i, h: (i, 0))]
    args = [cqn, wq, cos, sin]
    if absorbed:
        in_specs.append(pl.BlockSpec((hps, A_NOPE, KV_LORA), lambda i, h: (h, 0, 0)))
        args.append(wuk_t)
    return pl.pallas_call(
        functools.partial(_q_prep_kernel, absorbed=absorbed, hps=hps),
        grid=(m // tm, A_HEADS // hps),
        in_specs=in_specs,
        out_specs=pl.BlockSpec((hps, tm, width), lambda i, h: (h, i, 0)),
        out_shape=jax.ShapeDtypeStruct((A_HEADS, m, width), BF16),
        compiler_params=_params(2),
        name="q_prep",
    )(*args)


def _kv_prep_kernel(x_ref, wk_ref, wvt_ref, kr_ref, k_ref, vt_ref, *, blk, hps):
    x = x_ref[...]
    kr = kr_ref[...].astype(k_ref.dtype)
    for h in range(hps):
        k_ref[h, :, :A_NOPE] = _dot(x, wk_ref[h]).astype(k_ref.dtype)
        k_ref[h, :, A_NOPE:] = kr
        vt = _dot_nt(wvt_ref[h], x).astype(vt_ref.dtype)
        for c in range(x.shape[0] // blk):
            vt_ref[h, c, :A_VDIM, :] = vt[:, c * blk:(c + 1) * blk]
            vt_ref[h, c, A_VDIM:, :] = jnp.ones((A_VDIM, blk), vt_ref.dtype)


def kv_prep(ckv_b, krope, wk, wvt, blk):
    m = ckv_b.shape[0]
    tm = _tile(m, ROW_TILE, blk)
    hps = PREP_HEADS_PER_STEP
    assert A_HEADS % hps == 0
    return pl.pallas_call(
        functools.partial(_kv_prep_kernel, blk=blk, hps=hps),
        grid=(m // tm, A_HEADS // hps),
        in_specs=[pl.BlockSpec((tm, KV_LORA), lambda i, h: (i, 0)),
                  pl.BlockSpec((hps, KV_LORA, A_NOPE), lambda i, h: (h, 0, 0)),
                  pl.BlockSpec((hps, A_VDIM, KV_LORA), lambda i, h: (h, 0, 0)),
                  pl.BlockSpec((tm, A_ROPE), lambda i, h: (i, 0))],
        out_specs=[pl.BlockSpec((hps, tm, A_NOPE + A_ROPE), lambda i, h: (h, i, 0)),
                   pl.BlockSpec((hps, tm // blk, 2 * A_VDIM, blk), lambda i, h: (h, i, 0, 0))],
        out_shape=[jax.ShapeDtypeStruct((A_HEADS, m, A_NOPE + A_ROPE), BF16),
                   jax.ShapeDtypeStruct((A_HEADS, m // blk, 2 * A_VDIM, blk), BF16)],
        compiler_params=_params(2),
        name="kv_prep",
    )(ckv_b, wk, wvt, krope)


def _softmax_step(s, v, m, l, acc):
    m_new = jnp.maximum(m, jnp.max(s, axis=1, keepdims=True))
    p = jnp.exp(s - m_new)
    alpha = jnp.exp(m - m_new)
    l = alpha * l + jnp.sum(p, axis=1, keepdims=True)
    acc = alpha * acc + _dot(p.astype(BF16), v)
    return m_new, l, acc


def _flash_kernel(q_ref, k_ref, vt_ref, o_ref, m_ref, acc_ref, *, blk, hps):
    qi = pl.program_id(2)
    m_ref[...] = jnp.full_like(m_ref, -jnp.inf)
    acc_ref[...] = jnp.zeros_like(acc_ref)

    def block(kj, diagonal):
        start = pl.multiple_of(kj * blk, blk)
        scores = [_dot_nt(k_ref[h, pl.ds(start, blk), :], q_ref[h]) for h in range(hps)]
        if diagonal:
            key = lax.broadcasted_iota(jnp.int32, (blk, blk), 0)
            qry = lax.broadcasted_iota(jnp.int32, (blk, blk), 1)
            scores = [jnp.where(key <= qry, s, -jnp.inf) for s in scores]
        probs, alphas = [], []
        for h, s in enumerate(scores):
            m_old = m_ref[h]
            m_new = jnp.maximum(m_old, jnp.max(s, axis=0, keepdims=True))
            probs.append(jnp.exp(s - m_new).astype(BF16))
            alphas.append(jnp.exp(m_old - m_new))
            m_ref[h] = m_new
        for h in range(hps):
            acc_ref[h] = alphas[h] * acc_ref[h] + _dot(vt_ref[h, kj], probs[h])

    def body(kj, carry):
        block(kj, False)
        return carry

    lax.fori_loop(0, qi, body, 0)
    block(qi, True)
    for h in range(hps):
        acc = acc_ref[h]
        out_t = acc[:A_VDIM] / acc[A_VDIM:A_VDIM + 1]
        o_ref[:, h * A_VDIM:(h + 1) * A_VDIM] = out_t.T.astype(o_ref.dtype)


def flash_prompt(q, k, vt, batch, seq, blk):
    nq = seq // blk
    dqk = q.shape[2]
    hps = ATT_HEADS_PER_STEP
    assert A_HEADS % hps == 0
    return pl.pallas_call(
        functools.partial(_flash_kernel, blk=blk, hps=hps),
        grid=(batch, A_HEADS // hps, nq),
        in_specs=[pl.BlockSpec((hps, blk, dqk), lambda b, h, i: (h, b * nq + i, 0)),
                  pl.BlockSpec((hps, seq, dqk), lambda b, h, i: (h, b, 0)),
                  pl.BlockSpec((hps, nq, 2 * A_VDIM, blk), lambda b, h, i: (h, b, 0, 0))],
        out_specs=pl.BlockSpec((blk, hps * A_VDIM), lambda b, h, i: (b * nq + i, h)),
        out_shape=jax.ShapeDtypeStruct((batch * seq, A_HEADS * A_VDIM), BF16),
        scratch_shapes=[pltpu.VMEM((hps, 1, blk), F32), pltpu.VMEM((hps, 2 * A_VDIM, blk), F32)],
        compiler_params=_params(3),
        name="flash_prompt",
    )(q, k, vt)


def _paged_kernel(pt_ref, q_ref, *refs, n_steps, dec_seq):
    pp = PAGES_PER_STEP
    ckv_refs, kr_refs = refs[:pp], refs[pp:2 * pp]
    ckv_new_ref, kr_new_ref, o_ref, m_ref, l_ref, acc_ref = refs[2 * pp:]
    g = pl.program_id(1)

    @pl.when(g == 0)
    def _():
        m_ref[...] = jnp.full_like(m_ref, -jnp.inf)
        l_ref[...] = jnp.zeros_like(l_ref)
        acc_ref[...] = jnp.zeros_like(acc_ref)

    q = q_ref[...]
    q_lat, q_rope = q[:, :KV_LORA], q[:, KV_LORA:]
    grp = PAGES_PER_GROUP
    n_chain = pp // grp
    kcs, scores, probs, alphas = {}, {}, {}, {}

    def score(c):
        lo = c * grp
        kcs[c] = jnp.concatenate([r[...].astype(BF16) for r in ckv_refs[lo:lo + grp]], axis=0)
        krt = jnp.concatenate([r[...].astype(BF16) for r in kr_refs[lo:lo + grp]], axis=1)
        scores[c] = _dot_nt(q_lat, kcs[c]) + _dot(q_rope, krt)

    def stats(c):
        m_old = m_ref[c]
        m_new = jnp.maximum(m_old, jnp.max(scores[c], axis=1, keepdims=True))
        p = jnp.exp(scores[c] - m_new)
        alphas[c] = jnp.exp(m_old - m_new)
        l_ref[c] = alphas[c] * l_ref[c] + jnp.sum(p, axis=1, keepdims=True)
        m_ref[c] = m_new
        probs[c] = p.astype(BF16)

    def value(c):
        acc_ref[c] = alphas[c] * acc_ref[c] + _dot(probs[c], kcs[c])

    for c in range(n_chain + 2):
        if c < n_chain:
            score(c)
        if 0 <= c - 1 < n_chain:
            stats(c - 1)
        if 0 <= c - 2 < n_chain:
            value(c - 2)

    @pl.when(g == n_steps - 1)
    def _():
        def pad_rows(x):
            return jnp.concatenate([x, jnp.zeros((PAGE_SIZE - dec_seq, x.shape[1]), x.dtype)], axis=0)

        kcn = pad_rows(ckv_new_ref[...]).astype(BF16)
        krn = pad_rows(kr_new_ref[...]).astype(BF16)
        sn = _dot_nt(q_lat, kcn) + _dot_nt(q_rope, krn)
        tok = lax.broadcasted_iota(jnp.int32, sn.shape, 0) % dec_seq
        key = lax.broadcasted_iota(jnp.int32, sn.shape, 1)
        sn = jnp.where(key <= tok, sn, -jnp.inf)
        m, l, acc = _softmax_step(sn, kcn, m_ref[0], l_ref[0], acc_ref[0])
        for c in range(1, pp // grp):
            m_c = m_ref[c]
            m_new = jnp.maximum(m, m_c)
            a, b = jnp.exp(m - m_new), jnp.exp(m_c - m_new)
            l = a * l + b * l_ref[c]
            acc = a * acc + b * acc_ref[c]
            m = m_new
        o_ref[...] = (acc / l).astype(o_ref.dtype)


def paged_attention(q, pool_ckv, pool_kr, page_table, ckv_new, kr_new):
    nb, rows, dq = q.shape
    n_pages = page_table.shape[1]
    dec_seq = ckv_new.shape[1]
    pp = PAGES_PER_STEP
    assert n_pages % pp == 0 and pp % PAGES_PER_GROUP == 0
    n_steps = n_pages // pp
    page = pool_ckv.shape[1]
    assert page == PAGE_SIZE and pool_kr.shape[1:] == (A_ROPE, page)

    def page_map(i):
        return lambda b, g, pt: (pt[b * n_pages + g * pp + i], 0, 0)

    in_specs = [pl.BlockSpec((None, rows, dq), lambda b, g, pt: (b, 0, 0))]
    in_specs += [pl.BlockSpec((None, page, KV_LORA), page_map(i)) for i in range(pp)]
    in_specs += [pl.BlockSpec((None, A_ROPE, page), page_map(i)) for i in range(pp)]
    in_specs += [pl.BlockSpec((None, dec_seq, KV_LORA), lambda b, g, pt: (b, 0, 0)),
                 pl.BlockSpec((None, dec_seq, A_ROPE), lambda b, g, pt: (b, 0, 0))]
    grid_spec = pltpu.PrefetchScalarGridSpec(
        num_scalar_prefetch=1,
        grid=(nb, n_steps),
        in_specs=in_specs,
        out_specs=pl.BlockSpec((None, rows, KV_LORA), lambda b, g, pt: (b, 0, 0)),
        scratch_shapes=[pltpu.VMEM((pp // PAGES_PER_GROUP, rows, 1), F32),
                        pltpu.VMEM((pp // PAGES_PER_GROUP, rows, 1), F32),
                        pltpu.VMEM((pp // PAGES_PER_GROUP, rows, KV_LORA), F32)],
    )
    return pl.pallas_call(
        functools.partial(_paged_kernel, n_steps=n_steps, dec_seq=dec_seq),
        grid_spec=grid_spec,
        out_shape=jax.ShapeDtypeStruct((nb, rows, KV_LORA), BF16),
        compiler_params=_params(2),
        name="paged_attention",
    )(page_table.reshape(-1), q, *([pool_ckv] * pp), *([pool_kr] * pp), ckv_new, kr_new)


def _head_mm_kernel(x_ref, w_ref, o_ref):
    o_ref[...] = _dot(x_ref[...], w_ref[...]).astype(o_ref.dtype)


def head_matmul(x, w):
    nh, m, kd = x.shape
    n = w.shape[2]
    return pl.pallas_call(
        _head_mm_kernel,
        grid=(nh,),
        in_specs=[pl.BlockSpec((None, m, kd), lambda h: (h, 0, 0)),
                  pl.BlockSpec((None, kd, n), lambda h: (h, 0, 0))],
        out_specs=pl.BlockSpec((m, n), lambda h: (0, h)),
        out_shape=jax.ShapeDtypeStruct((m, nh * n), BF16),
        compiler_params=_params(1),
        name="head_matmul",
    )(x, w)


def _mlstm_kernel(q_ref, k_ref, v_ref, if_ref, mo_ref, bias_ref, gmh_ref, c0_ref, n0_ref, m0_ref,
                  o_ref, c_ref, n_ref, m_ref, *, rows, lp, nseq):
    chunk = pl.program_id(1)

    @pl.when(chunk == 0)
    def _():
        c_ref[...] = c0_ref[...]
        n_ref[...] = n0_ref[...]
        m_ref[...] = m0_ref[...]

    def pad(x):
        if rows == lp:
            return x
        return jnp.concatenate([x, jnp.zeros((lp - rows, x.shape[1]), x.dtype)], axis=0)

    t_idx = lax.broadcasted_iota(jnp.int32, (lp, lp), 0)
    s_idx = lax.broadcasted_iota(jnp.int32, (lp, lp), 1)
    causal = s_idx <= t_idx
    diag = s_idx == t_idx
    valid = lax.broadcasted_iota(jnp.int32, (lp, 1), 0) < rows
    for sq, h in [(sq, h) for sq in range(nseq) for h in range(M_HEADS)]:
        r0 = sq * rows
        gates = pad(if_ref[r0:r0 + rows, :] + bias_ref[...])
        q = pad(q_ref[r0:r0 + rows, h * M_DQK:(h + 1) * M_DQK]) * (M_DQK ** -0.5)
        k = pad(k_ref[r0:r0 + rows, h * M_DQK:(h + 1) * M_DQK])
        v = pad(v_ref[r0:r0 + rows, h * M_DV:(h + 1) * M_DV])
        i_col = gates[:, h:h + 1]
        f_pre = gates[:, M_HEADS + h:M_HEADS + h + 1]
        f_col = jnp.minimum(f_pre, 0.0) - jnp.log1p(jnp.exp(-jnp.abs(f_pre)))
        if rows != lp:
            i_col = jnp.where(valid, i_col, -jnp.inf)
            f_col = jnp.where(valid, f_col, 0.0)

        f_row = jnp.sum(jnp.where(diag, f_col, 0.0), axis=0, keepdims=True)
        i_row = jnp.sum(jnp.where(diag, i_col, 0.0), axis=0, keepdims=True)
        b_col = jnp.sum(jnp.where(causal, f_row, 0.0), axis=1, keepdims=True)
        b_row = jnp.sum(jnp.where(t_idx <= s_idx, f_col, 0.0), axis=0, keepdims=True)

        m_prev = m_ref[sq, h]
        c_prev = c_ref[sq, h]
        n_prev = n_ref[sq, h]
        a_col = b_col + m_prev
        dmat = jnp.where(causal, b_col - b_row + i_row, -jnp.inf)
        mt = jnp.maximum(a_col, jnp.max(dmat, axis=1, keepdims=True))
        w_inter = jnp.exp(a_col - mt)
        qb, kb, vb = q.astype(BF16), k.astype(BF16), v.astype(BF16)
        qk = _dot_nt(qb, kb) * jnp.exp(dmat - mt)
        num = _dot(qk.astype(BF16), vb) + w_inter * _dot(qb, c_prev.astype(BF16))
        den = jnp.sum(qk, axis=1, keepdims=True) + w_inter * jnp.sum(q * n_prev, axis=1, keepdims=True)
        hid = num / jnp.maximum(jnp.abs(den), jnp.exp(-mt))

        b_last = b_col[lp - 1:lp, :]
        g_col = b_last - b_col + i_col
        m_new = jnp.maximum(b_last + m_prev, jnp.max(g_col, axis=0, keepdims=True))
        decay = jnp.exp(b_last + m_prev - m_new)
        kw = jnp.exp(g_col - m_new) * k
        c_ref[sq, h] = decay * c_prev + _dot_tn(kw.astype(BF16), vb)
        n_ref[sq, h] = decay * n_prev + jnp.sum(kw, axis=0, keepdims=True)
        m_ref[sq, h] = m_new

        hn = hid * lax.rsqrt(jnp.mean(hid * hid, axis=1, keepdims=True) + EPS) * gmh_ref[h]
        out = jax.nn.sigmoid(pad(mo_ref[r0:r0 + rows, h * M_DV:(h + 1) * M_DV])) * hn
        o_ref[r0:r0 + rows, h * M_DV:(h + 1) * M_DV] = out[:rows].astype(o_ref.dtype)


def mlstm(proj, lay, bias_row, g_mh, c0, n0, m0, batch, seq):
    rows = _tile(seq, MLSTM_CHUNK, 8)
    lp = max(rows, MLSTM_MIN_ROWS)
    nc = seq // rows
    nseq = MLSTM_SEQS_PER_STEP if (nc == 1 and batch % MLSTM_SEQS_PER_STEP == 0) else 1
    tr = nseq * rows
    hq, hv = M_HEADS * M_DQK, M_HEADS * M_DV
    oq, ok, ov, oi, oo = (lay[n][0] for n in ("mq", "mk", "mv", "mif", "mo"))
    assert oq % hq == 0 and ok % hq == 0 and ov % hv == 0 and oo % hv == 0 and oi % LANES == 0

    def rowmap(col_block):
        return lambda b, c: (b * nc + c, col_block)

    def state_spec(*dims):
        return pl.BlockSpec((nseq, M_HEADS) + dims, lambda b, c: (b, 0, 0, 0))

    out, c1, n1, m1 = pl.pallas_call(
        functools.partial(_mlstm_kernel, rows=rows, lp=lp, nseq=nseq),
        grid=(batch // nseq, nc),
        in_specs=[pl.BlockSpec((tr, hq), rowmap(oq // hq)),
                  pl.BlockSpec((tr, hq), rowmap(ok // hq)),
                  pl.BlockSpec((tr, hv), rowmap(ov // hv)),
                  pl.BlockSpec((tr, LANES), rowmap(oi // LANES)),
                  pl.BlockSpec((tr, hv), rowmap(oo // hv)),
                  pl.BlockSpec((1, LANES), lambda b, c: (0, 0)),
                  pl.BlockSpec((M_HEADS, 1, M_DV), lambda b, c: (0, 0, 0)),
                  state_spec(M_DQK, M_DV), state_spec(1, M_DQK), state_spec(1, 1)],
        out_specs=[pl.BlockSpec((tr, hv), lambda b, c: (b * nc + c, 0)),
                   state_spec(M_DQK, M_DV), state_spec(1, M_DQK), state_spec(1, 1)],
        out_shape=[jax.ShapeDtypeStruct((batch * seq, hv), BF16),
                   jax.ShapeDtypeStruct((batch, M_HEADS, M_DQK, M_DV), F32),
                   jax.ShapeDtypeStruct((batch, M_HEADS, 1, M_DQK), F32),
                   jax.ShapeDtypeStruct((batch, M_HEADS, 1, 1), F32)],
        compiler_params=_params(2),
        name="mlstm",
    )(proj, proj, proj, proj, proj, bias_row, g_mh.reshape(M_HEADS, 1, M_DV),
      c0, n0.reshape(batch, M_HEADS, 1, M_DQK), m0.reshape(batch, M_HEADS, 1, 1))
    return out, c1, n1.reshape(batch, M_HEADS, M_DQK), m1.reshape(batch, M_HEADS)


def _ffn_up_kernel(x_ref, wg_ref, wv_ref, c0_ref, w_ref, b_ref, act_ref, tail_ref, wg_bf, wv_bf, *carry,
                   nb, tiles_per_seq):
    i = pl.program_id(1)
    _cast_weights_once([wg_ref, wv_ref], [wg_bf, wv_bf])
    x = x_ref[...]
    tm, tc = act_ref.shape
    g = _dot(x, wg_bf[...]).reshape(nb, tm // nb, tc)
    val = _dot(x, wv_bf[...]).reshape(nb, tm // nb, tc)
    st = tm // nb
    last_rows = g[:, st - (CONV_W - 1):, :]
    prev = c0_ref[...]
    if tiles_per_seq > 1:
        carry_ref = carry[0]

        @pl.when(i == 0)
        def _():
            carry_ref[...] = jnp.zeros_like(carry_ref)

        prev = jnp.where(i % tiles_per_seq == 0, prev, carry_ref[...])
        carry_ref[...] = last_rows
    t = lax.broadcasted_iota(jnp.int32, g.shape, 1)
    prev1 = jnp.where(t == 0, prev[:, 1:2, :], pltpu.roll(g, 1, 1))
    prev2 = jnp.where(t == 0, prev[:, 0:1, :], jnp.where(t == 1, prev[:, 1:2, :], pltpu.roll(g, 2, 1)))
    w = w_ref[...]
    conv = b_ref[...] + w[0:1, :] * prev2 + w[1:2, :] * prev1 + w[2:3, :] * g
    act_ref[...] = (jax.nn.gelu(conv, approximate=True) * val).reshape(tm, tc).astype(act_ref.dtype)
    tail_ref[...] = last_rows


def ffn_up_convglu(x, w_up, conv0, w_conv, b_conv, batch, seq):
    assert CONV_W == 3 and seq >= CONV_W - 1
    m, kd = x.shape
    tc = _tile(D_FF, 256)
    ncol = D_FF // tc
    if seq >= ROW_TILE:
        tm, nb = _tile(seq, ROW_TILE, 8), 1
    else:
        nb = max(1, min(batch, ROW_TILE // seq))
        assert batch % nb == 0
        tm = nb * seq
    tps = seq // (tm // nb)
    assert w_up.dtype == F32
    scratch = [pltpu.VMEM((kd, tc), BF16), pltpu.VMEM((kd, tc), BF16)]
    if tps > 1:
        scratch.append(pltpu.VMEM((nb, CONV_W - 1, tc), F32))
    act, tail = pl.pallas_call(
        functools.partial(_ffn_up_kernel, nb=nb, tiles_per_seq=tps),
        grid=(ncol, m // tm),
        in_specs=[pl.BlockSpec((tm, kd), lambda j, i: (i, 0)),
                  pl.BlockSpec((kd, tc), lambda j, i: (0, j)),
                  pl.BlockSpec((kd, tc), lambda j, i: (0, j + ncol)),
                  pl.BlockSpec((nb, CONV_W - 1, tc), lambda j, i: (i // tps, 0, j)),
                  pl.BlockSpec((CONV_W, tc), lambda j, i: (0, j)),
                  pl.BlockSpec((1, tc), lambda j, i: (0, j))],
        out_specs=[pl.BlockSpec((tm, tc), lambda j, i: (i, j)),
                   pl.BlockSpec((nb, CONV_W - 1, tc), lambda j, i: (i, 0, j))],
        out_shape=[jax.ShapeDtypeStruct((m, D_FF), BF16),
                   jax.ShapeDtypeStruct((batch * tps, CONV_W - 1, D_FF), F32)],
        scratch_shapes=scratch,
        compiler_params=_params(2),
        name="ffn_up_convglu",
    )(x, w_up, w_up, conv0, w_conv, b_conv.reshape(1, D_FF))
    return act, tail.reshape(batch, tps, CONV_W - 1, D_FF)[:, tps - 1]


_MODE_COPY, _MODE_DUP_ROPE, _MODE_GATES, _MODE_ZERO = 0, 1, 2, 3


def _relayout_kernel(off_ref, mode_ref, src_ref, o_ref):
    mode = mode_ref[pl.program_id(0)]
    row = lax.broadcasted_iota(jnp.int32, src_ref.shape, 0)

    @pl.when(mode == _MODE_COPY)
    def _():
        o_ref[...] = src_ref[...].T.astype(o_ref.dtype)

    @pl.when(mode == _MODE_DUP_ROPE)
    def _():
        kr = src_ref[:A_ROPE, :]
        o_ref[...] = jnp.concatenate([kr, kr], axis=0).T.astype(o_ref.dtype)

    @pl.when(mode == _MODE_GATES)
    def _():
        o_ref[...] = jnp.where(row < 2 * M_HEADS, src_ref[...], 0.0).T.astype(o_ref.dtype)

    @pl.when(mode == _MODE_ZERO)
    def _():
        o_ref[...] = jnp.zeros_like(o_ref)


def relayout_w_in(w_in_t, lay, total, src_off):
    assert 2 * A_ROPE == LANES and 2 * M_HEADS <= LANES
    n_tiles = total // LANES
    off, mode = np.zeros(n_tiles, np.int32), np.full(n_tiles, _MODE_ZERO, np.int32)
    groups = [(lay[n][0], lay[n][1], src_off[n], _MODE_COPY) for n in ("cq", "mq", "mk", "ckv", "mv", "mo", "ga", "gb")]
    groups += [(*lay["kr2"], src_off["kr"], _MODE_DUP_ROPE), (*lay["mif"], src_off["mi"], _MODE_GATES)]
    n_src, rows = w_in_t.shape
    for dst, width, s0, md in groups:
        assert dst % LANES == 0 and width % LANES == 0 and s0 % 8 == 0
        for t in range(width // LANES):
            off[dst // LANES + t], mode[dst // LANES + t] = s0 + t * LANES, md
    assert int(off.max()) + LANES <= n_src
    return pl.pallas_call(
        _relayout_kernel,
        grid_spec=pltpu.PrefetchScalarGridSpec(
            num_scalar_prefetch=2,
            grid=(n_tiles,),
            in_specs=[pl.BlockSpec((pl.Element(LANES), pl.Element(rows)),
                                   lambda t, off8, mode: (pl.multiple_of(off8[t] * 8, 8), 0))],
            out_specs=pl.BlockSpec((rows, LANES), lambda t, off, mode: (0, t)),
        ),
        out_shape=jax.ShapeDtypeStruct((rows, total), BF16),
        compiler_params=_params(1),
        name="relayout_w_in",
    )(jnp.asarray(off // 8), jnp.asarray(mode), w_in_t)


def _prepare_weights(w):
    lay, total = _padded_layout()
    offs = np.concatenate([[0], np.cumsum([Q_LORA, KV_LORA, A_ROPE, M_HEADS * M_DQK, M_HEADS * M_DQK,
                                            M_HEADS * M_DV, M_HEADS, M_HEADS, M_HEADS * M_DV, D_MODEL, D_MODEL])])
    src = {n: (int(offs[i]), int(offs[i + 1])) for i, n in enumerate(
        ["cq", "ckv", "kr", "mq", "mk", "mv", "mi", "mf", "mo", "ga", "gb"])}
    assert src["mf"][0] == src["mi"][1]
    w_in_p = relayout_w_in(jnp.swapaxes(w["w_in"], 0, 1), lay, total, {n: s[0] for n, s in src.items()})

    w_uq = w["w_uq"]
    rope_cols = w_uq[..., A_NOPE:]
    wq = jnp.concatenate([w_uq[..., :A_NOPE], rope_cols, rope_cols], axis=-1).transpose(1, 0, 2).astype(BF16)
    bias_row = jnp.concatenate([w["b_i"].astype(F32), w["b_f"].astype(F32),
                                jnp.zeros((LANES - 2 * M_HEADS,), F32)]).reshape(1, LANES)
    return {
        "lay": lay, "n_proj": total, "w_in_p": w_in_p, "wq": wq,
        "wuk_t": w["w_uk"].transpose(1, 2, 0).astype(BF16),
        "wk": w["w_uk"].transpose(1, 0, 2).astype(BF16),
        "wv": w["w_uv"].transpose(1, 0, 2).astype(BF16),
        "wvt": w["w_uv"].transpose(1, 2, 0).astype(BF16),
        "bias_row": bias_row,
        "w_br_a": w["w_br_a"].astype(F32), "w_br_b": w["w_br_b"].astype(F32),
        "w_o": w["w_o"].astype(F32), "w_up": w["w_up"].astype(F32), "w_down": w["w_down"].astype(BF16),
        "w_ple_gate": w["w_ple_gate"].astype(F32), "w_ple_proj": w["w_ple_proj"].astype(F32),
    }


def _rope_tables(pos, batch):
    half = A_ROPE // 2
    inv = jnp.power(ROPE_THETA, -jnp.arange(half, dtype=F32) / half)
    ang = pos[:, None] * inv[None, :]
    cos, sin = jnp.cos(ang), jnp.sin(ang)
    reps = LANES // A_ROPE
    cos_t = jnp.tile(jnp.concatenate([cos, cos], axis=1), (batch, reps))
    sin_t = jnp.tile(jnp.concatenate([-sin, sin], axis=1), (batch, reps))
    return cos_t, sin_t


def _hybrid_layer(x, pe, pos, w, pw, c0, n0, m0, conv0, paged):
    batch, seq, _ = x.shape
    m = batch * seq
    lay = pw["lay"]
    x2 = x.reshape(m, D_MODEL)
    cos, sin = _rope_tables(pos, batch)

    h = rmsnorm_rows(x2, w["g_mix"], BF16)
    proj = fused_matmul([(h, pw["w_in_p"])], [], _ep_plain, pw["n_proj"], F32,
                        tn=_tile(pw["n_proj"], 1024), name="in_proj")
    cqn, c_kv, ckv_b, k_rope = mla_prep(proj, lay, w["g_qa"], w["g_kva"], cos, sin)

    if paged is None:
        q = q_prep(cqn, pw["wq"], cos, sin)
        blk = _tile(seq, ATT_BLOCK)
        k, vt = kv_prep(ckv_b, k_rope, pw["wk"], pw["wvt"], blk)
        a_out = flash_prompt(q, k, vt, batch, seq, blk)
    else:
        pool_ckv, pool_kr, page_table = paged
        q = q_prep(cqn, pw["wq"], cos, sin, pw["wuk_t"])
        dq = q.shape[2]
        q = q.reshape(A_HEADS, batch, seq, dq).transpose(1, 0, 2, 3).reshape(batch, A_HEADS * seq, dq)
        o_lat = paged_attention(q, pool_ckv, pool_kr, page_table,
                                c_kv.reshape(batch, seq, KV_LORA), k_rope.reshape(batch, seq, A_ROPE))
        o_lat = o_lat.reshape(batch, A_HEADS, seq, KV_LORA).transpose(1, 0, 2, 3).reshape(A_HEADS, m, KV_LORA)
        a_out = head_matmul(o_lat, pw["wv"])

    b_out, c1, n1, m1 = mlstm(proj, lay, pw["bias_row"], w["g_mh"], c0, n0, m0, batch, seq)

    merged = fused_matmul([(a_out, pw["w_br_a"]), (b_out, pw["w_br_b"])],
                          [(proj, lay["ga"][0]), (proj, lay["gb"][0])],
                          _ep_gated_merge, D_MODEL, BF16, name="branch_merge")
    x2 = fused_matmul([(merged, pw["w_o"])], [(x2, 0)], _ep_residual, D_MODEL, F32, name="out_proj")

    hf = rmsnorm_rows(x2, w["g_ffn"], BF16)
    act, conv_tail = ffn_up_convglu(hf, pw["w_up"], conv0.astype(F32), w["w_conv"], w["b_conv"], batch, seq)
    x2 = fused_matmul([(act, pw["w_down"])], [(x2, 0)], _ep_residual, D_MODEL, F32,
                      tn=_tile(D_MODEL, 512), tk=_tile(D_FF, D_FF // 2), name="ffn_down")

    hp = rmsnorm_rows(x2, w["g_ple"], BF16)
    x2 = fused_matmul([(hp, pw["w_ple_gate"]), (pe.reshape(m, PLE_DIM).astype(BF16), pw["w_ple_proj"])],
                      [(x2, 0)], _ep_ple, D_MODEL, F32, name="ple")
    return (x2, c_kv.reshape(batch, seq, KV_LORA), k_rope.reshape(batch, seq, A_ROPE), c1, n1, m1, conv_tail)


def kernel(x_prompt, x_sample, cache_ckv, cache_krope, state_C, state_n, state_m, state_conv, page_table,
           p_prompt, p_sample, g_mix, w_in, b_i, b_f, g_qa, w_uq, g_kva, w_uk, w_uv, g_mh, w_br_a, w_br_b,
           w_o, g_ffn, w_up, w_conv, b_conv, w_down, g_ple, w_ple_gate, w_ple_proj, g_final):
    bp, sp = x_prompt.shape[0], x_prompt.shape[1]
    bs, ss = x_sample.shape[0], x_sample.shape[1]
    depth = w_in.shape[0]
    past = page_table.shape[1] * cache_ckv.shape[2]
    pos_p = jnp.arange(sp, dtype=F32)
    pos_s = past + jnp.arange(ss, dtype=F32)
    c0p = jnp.zeros((bp, M_HEADS, M_DQK, M_DV), F32)
    n0p = jnp.zeros((bp, M_HEADS, M_DQK), F32)
    m0p = jnp.zeros((bp, M_HEADS), F32)
    conv0p = jnp.zeros((bp, CONV_W - 1, D_FF), F32)
    names = ["g_mix", "w_in", "b_i", "b_f", "g_qa", "w_uq", "g_kva", "w_uk", "w_uv", "g_mh", "w_br_a", "w_br_b",
             "w_o", "g_ffn", "w_up", "w_conv", "b_conv", "w_down", "g_ple", "w_ple_gate", "w_ple_proj"]
    stacked = [g_mix, w_in, b_i, b_f, g_qa, w_uq, g_kva, w_uk, w_uv, g_mh, w_br_a, w_br_b,
               w_o, g_ffn, w_up, w_conv, b_conv, w_down, g_ple, w_ple_gate, w_ple_proj]
    xp, xs = x_prompt, x_sample
    outs_p = [[] for _ in range(6)]
    outs_s = [[] for _ in range(6)]
    for l in range(depth):
        w = {n: a[l] for n, a in zip(names, stacked)}
        pw = _prepare_weights(w)
        xp2, *new_p = _hybrid_layer(xp, p_prompt[l], pos_p, w, pw, c0p, n0p, m0p, conv0p, None)
        xs2, *new_s = _hybrid_layer(xs, p_sample[l], pos_s, w, pw, state_C[l], state_n[l], state_m[l],
                                    state_conv[l],
                                    (cache_ckv[l], jnp.swapaxes(cache_krope[l], 1, 2), page_table))
        xp, xs = xp2.reshape(xp.shape), xs2.reshape(xs.shape)
        for j in range(6):
            outs_p[j].append(new_p[j])
            outs_s[j].append(new_s[j])
    stack_p = [jnp.stack(o) for o in outs_p]
    stack_s = [jnp.stack(o) for o in outs_s]
    y_prompt = rmsnorm_rows(xp.reshape(bp * sp, D_MODEL), g_final, F32).reshape(xp.shape)
    y_sample = rmsnorm_rows(xs.reshape(bs * ss, D_MODEL), g_final, F32).reshape(xs.shape)
    return (y_prompt, y_sample, *stack_p, *stack_s)
```

```python
import functools

import jax
import jax.numpy as jnp
import numpy as np
from jax import lax
from jax.experimental import pallas as pl
from jax.experimental.pallas import tpu as pltpu

D_MODEL = 4096
DEPTH = 1
PAGE_SIZE = 128
A_HEADS = 16
A_NOPE = 128
A_ROPE = 64
A_VDIM = 128
Q_LORA = 1024
KV_LORA = 512
ROPE_THETA = 10000.0
A_SCALE = (A_NOPE + A_ROPE) ** -0.5
M_HEADS = 4
M_DQK = 256
M_DV = 512
D_FF = 11008
CONV_W = 3
PLE_DIM = 256
EPS = 1e-6

LANES = 128
VMEM_LIMIT = 56 * 1024 * 1024
ROW_TILE = 1024
COL_TILE = 512
MLSTM_CHUNK = 256
MLSTM_MIN_ROWS = 128
MLSTM_SEQS_PER_STEP = 2
ATT_BLOCK = 256
ATT_HEADS_PER_STEP = 8
PREP_HEADS_PER_STEP = 4
PAGES_PER_STEP = 64
PAGES_PER_GROUP = 16

F32 = jnp.float32
BF16 = jnp.bfloat16


def _params(n_grid):
    return pltpu.CompilerParams(dimension_semantics=("arbitrary",) * n_grid,
                                vmem_limit_bytes=VMEM_LIMIT)


def _tile(n, pref, unit=LANES):
    if n <= pref:
        return n
    best = None
    for t in range(unit, pref + 1, unit):
        if n % t == 0:
            best = t
    assert best is not None, (n, pref)
    return best


def _padded_layout():
    hq, hv = M_HEADS * M_DQK, M_HEADS * M_DV
    widths = [("cq", Q_LORA), ("mq", hq), ("mk", hq), ("ckv", KV_LORA), ("kr2", 2 * A_ROPE), ("mif", LANES)]
    off, lay = 0, {}
    for name, w in widths:
        lay[name] = (off, w)
        off += w
    off = -(-off // hv) * hv
    for name, w in [("mv", hv), ("mo", hv), ("ga", D_MODEL), ("gb", D_MODEL)]:
        lay[name] = (off, w)
        off += w
    return lay, off


def _dot(a, b):
    return jnp.dot(a, b, preferred_element_type=F32)


def _dot_nt(a, b):
    return lax.dot_general(a, b, (((1,), (1,)), ((), ())), preferred_element_type=F32)


def _dot_tn(a, b):
    return lax.dot_general(a, b, (((0,), (0,)), ((), ())), preferred_element_type=F32)


def _rmsnorm_kernel(x_ref, g_ref, o_ref):
    x = x_ref[...]
    y = x * lax.rsqrt(jnp.mean(x * x, axis=-1, keepdims=True) + EPS)
    o_ref[...] = (y * g_ref[...]).astype(o_ref.dtype)


def rmsnorm_rows(x, g, out_dtype):
    m, d = x.shape
    tm = _tile(m, 512, 8)
    return pl.pallas_call(
        _rmsnorm_kernel,
        grid=(m // tm,),
        in_specs=[pl.BlockSpec((tm, d), lambda i: (i, 0)),
                  pl.BlockSpec((1, d), lambda i: (0, 0))],
        out_specs=pl.BlockSpec((tm, d), lambda i: (i, 0)),
        out_shape=jax.ShapeDtypeStruct((m, d), out_dtype),
        compiler_params=_params(1),
        name="rmsnorm",
    )(x, g.reshape(1, d).astype(F32))


def _cast_weights_once(w_refs, cast_refs):
    @pl.when(pl.program_id(1) == 0)
    def _():
        for w, c in zip(w_refs, cast_refs):
            c[...] = w[...].astype(c.dtype)


def _matmul_kernel(*refs, n_pairs, n_extra, epilogue, nk, n_cast):
    pairs = [(refs[2 * p], refs[2 * p + 1]) for p in range(n_pairs)]
    extras = refs[2 * n_pairs:2 * n_pairs + n_extra]
    o_ref = refs[2 * n_pairs + n_extra]
    if nk == 1:
        if n_cast:
            cast_refs = refs[2 * n_pairs + n_extra + 1:]
            _cast_weights_once([w for _, w in pairs], cast_refs)
            pairs = [(x, c) for (x, _), c in zip(pairs, cast_refs)]
        accs = [_dot(x[...], w[...]) for x, w in pairs]
        o_ref[...] = epilogue(accs, [e[...] for e in extras]).astype(o_ref.dtype)
        return
    acc_ref = refs[2 * n_pairs + n_extra + 1]
    k = pl.program_id(2)
    x, w = pairs[0]

    @pl.when(k == 0)
    def _():
        acc_ref[...] = jnp.zeros_like(acc_ref)

    acc_ref[...] += _dot(x[...], w[...])

    @pl.when(k == nk - 1)
    def _():
        o_ref[...] = epilogue([acc_ref[...]], [e[...] for e in extras]).astype(o_ref.dtype)


def fused_matmul(pairs, extras, epilogue, n_out, out_dtype, *, tm=None, tn=None, tk=None, name="matmul"):
    m = pairs[0][0].shape[0]
    tm = tm or _tile(m, ROW_TILE, 8)
    tn = tn or _tile(n_out, COL_TILE)
    kdim = pairs[0][0].shape[1]
    nk = 1 if tk is None else kdim // tk
    assert nk == 1 or len(pairs) == 1
    cast = any(w.dtype != BF16 for _, w in pairs)
    assert not cast or (nk == 1 and all(w.dtype == F32 for _, w in pairs))

    def order(f):
        return (lambda j, i: f(i, j)) if cast else f

    in_specs, args = [], []
    for x, w in pairs:
        kd = x.shape[1]
        if nk == 1:
            in_specs += [pl.BlockSpec((tm, kd), order(lambda i, j: (i, 0))),
                         pl.BlockSpec((kd, tn), order(lambda i, j: (0, j)))]
        else:
            in_specs += [pl.BlockSpec((tm, tk), lambda i, j, k: (i, k)),
                         pl.BlockSpec((tk, tn), lambda i, j, k: (k, j))]
        args += [x, w]
    for arr, off in extras:
        assert off % tn == 0
        ob = off // tn
        if nk == 1:
            in_specs.append(pl.BlockSpec((tm, tn), order(lambda i, j, ob=ob: (i, j + ob))))
        else:
            in_specs.append(pl.BlockSpec((tm, tn), lambda i, j, k, ob=ob: (i, j + ob)))
        args.append(arr)
    if nk == 1:
        grid = (n_out // tn, m // tm) if cast else (m // tm, n_out // tn)
        out_spec = pl.BlockSpec((tm, tn), order(lambda i, j: (i, j)))
        scratch = [pltpu.VMEM((x.shape[1], tn), BF16) for x, _ in pairs] if cast else []
    else:
        grid = (m // tm, n_out // tn, nk)
        out_spec = pl.BlockSpec((tm, tn), lambda i, j, k: (i, j))
        scratch = [pltpu.VMEM((tm, tn), F32)]
    return pl.pallas_call(
        functools.partial(_matmul_kernel, n_pairs=len(pairs), n_extra=len(extras), epilogue=epilogue, nk=nk,
                          n_cast=len(pairs) if cast else 0),
        grid=grid,
        in_specs=in_specs,
        out_specs=out_spec,
        out_shape=jax.ShapeDtypeStruct((m, n_out), out_dtype),
        scratch_shapes=scratch,
        compiler_params=_params(len(grid)),
        name=name,
    )(*args)


def _ep_plain(accs, extras):
    return accs[0]


def _ep_residual(accs, extras):
    return extras[0] + accs[0]


def _ep_gated_merge(accs, extras):
    return jax.nn.sigmoid(extras[0]) * accs[0] + jax.nn.sigmoid(extras[1]) * accs[1]


def _ep_ple(accs, extras):
    return extras[0] + jax.nn.sigmoid(accs[0]) * accs[1]


def _rope_mix(x, cos, sin):
    return x * cos + pltpu.roll(x, A_ROPE // 2, 1) * sin


def _mla_prep_kernel(cq_ref, ckv_ref, kr_ref, gq_ref, gkv_ref, cos_ref, sin_ref,
                     cqn_ref, ckv_out_ref, ckvb_ref, krope_ref):
    cq = cq_ref[...]
    cqn = cq * lax.rsqrt(jnp.mean(cq * cq, axis=-1, keepdims=True) + EPS) * gq_ref[...]
    cqn_ref[...] = cqn.astype(cqn_ref.dtype)
    ckv = ckv_ref[...]
    ckvn = ckv * lax.rsqrt(jnp.mean(ckv * ckv, axis=-1, keepdims=True) + EPS) * gkv_ref[...]
    ckv_out_ref[...] = ckvn
    ckvb_ref[...] = ckvn.astype(ckvb_ref.dtype)
    krope_ref[...] = _rope_mix(kr_ref[...], cos_ref[...], sin_ref[...])[:, :A_ROPE]


def mla_prep(proj, lay, g_qa, g_kva, cos, sin):
    m = proj.shape[0]
    tm = _tile(m, 512, 8)
    (o_cq, w_cq), (o_ckv, w_ckv), (o_kr, w_kr) = lay["cq"], lay["ckv"], lay["kr2"]
    assert o_cq % w_cq == 0 and o_ckv % w_ckv == 0 and o_kr % w_kr == 0
    return pl.pallas_call(
        _mla_prep_kernel,
        grid=(m // tm,),
        in_specs=[pl.BlockSpec((tm, w_cq), lambda i: (i, o_cq // w_cq)),
                  pl.BlockSpec((tm, w_ckv), lambda i: (i, o_ckv // w_ckv)),
                  pl.BlockSpec((tm, w_kr), lambda i: (i, o_kr // w_kr)),
                  pl.BlockSpec((1, w_cq), lambda i: (0, 0)),
                  pl.BlockSpec((1, w_ckv), lambda i: (0, 0)),
                  pl.BlockSpec((tm, LANES), lambda i: (i, 0)),
                  pl.BlockSpec((tm, LANES), lambda i: (i, 0))],
        out_specs=[pl.BlockSpec((tm, w_cq), lambda i: (i, 0)),
                   pl.BlockSpec((tm, w_ckv), lambda i: (i, 0)),
                   pl.BlockSpec((tm, w_ckv), lambda i: (i, 0)),
                   pl.BlockSpec((tm, A_ROPE), lambda i: (i, 0))],
        out_shape=[jax.ShapeDtypeStruct((m, w_cq), BF16),
                   jax.ShapeDtypeStruct((m, w_ckv), F32),
                   jax.ShapeDtypeStruct((m, w_ckv), BF16),
                   jax.ShapeDtypeStruct((m, A_ROPE), F32)],
        compiler_params=_params(1),
        name="mla_prep",
    )(proj, proj, proj, g_qa.reshape(1, -1), g_kva.reshape(1, -1), cos, sin)


def _q_prep_kernel(x_ref, w_ref, cos_ref, sin_ref, *rest, absorbed, hps):
    x, cos, sin = x_ref[...], cos_ref[...], sin_ref[...]
    for h in range(hps):
        acc = _dot(x, w_ref[h])
        nope = (acc[:, :A_NOPE] * A_SCALE).astype(BF16)
        rot = _rope_mix(acc[:, A_NOPE:], cos, sin)[:, :A_ROPE] * A_SCALE
        if absorbed:
            wuk_ref, o_ref = rest
            o_ref[h, :, :KV_LORA] = _dot(nope, wuk_ref[h]).astype(o_ref.dtype)
            o_ref[h, :, KV_LORA:] = rot.astype(o_ref.dtype)
        else:
            (o_ref,) = rest
            o_ref[h, :, :A_NOPE] = nope
            o_ref[h, :, A_NOPE:] = rot.astype(o_ref.dtype)


def q_prep(cqn, wq, cos, sin, wuk_t=None):
    m, kq = cqn.shape
    tm = _tile(m, ROW_TILE, 8)
    wcols = wq.shape[2]
    hps = PREP_HEADS_PER_STEP
    assert A_HEADS % hps == 0
    absorbed = wuk_t is not None
    width = (KV_LORA if absorbed else A_NOPE) + A_ROPE
    in_specs = [pl.BlockSpec((tm, kq), lambda i, h: (i, 0)),
                pl.BlockSpec((hps, kq, wcols), lambda i, h: (h, 0, 0)),
                pl.BlockSpec((tm, LANES), lambda i, h: (i, 0)),
                pl.BlockSpec((tm, LANES), lambda i, h: (i, 0))]
    args = [cqn, wq, cos, sin]
    if absorbed:
        in_specs.append(pl.BlockSpec((hps, A_NOPE, KV_LORA), lambda i, h: (h, 0, 0)))
        args.append(wuk_t)
    return pl.pallas_call(
        functools.partial(_q_prep_kernel, absorbed=absorbed, hps=hps),
        grid=(m // tm, A_HEADS // hps),
        in_specs=in_specs,
        out_specs=pl.BlockSpec((hps, tm, width), lambda i, h: (h, i, 0)),
        out_shape=jax.ShapeDtypeStruct((A_HEADS, m, width), BF16),
        compiler_params=_params(2),
        name="q_prep",
    )(*args)


def _kv_prep_kernel(x_ref, wk_ref, wvt_ref, kr_ref, k_ref, vt_ref, *, blk, hps):
    x = x_ref[...]
    kr = kr_ref[...].astype(k_ref.dtype)
    for h in range(hps):
        k_ref[h, :, :A_NOPE] = _dot(x, wk_ref[h]).astype(k_ref.dtype)
        k_ref[h, :, A_NOPE:] = kr
        vt = _dot_nt(wvt_ref[h], x).astype(vt_ref.dtype)
        for c in range(x.shape[0] // blk):
            vt_ref[h, c, :A_VDIM, :] = vt[:, c * blk:(c + 1) * blk]
            vt_ref[h, c, A_VDIM:, :] = jnp.ones((A_VDIM, blk), vt_ref.dtype)


def kv_prep(ckv_b, krope, wk, wvt, blk):
    m = ckv_b.shape[0]
    tm = _tile(m, ROW_TILE, blk)
    hps = PREP_HEADS_PER_STEP
    assert A_HEADS % hps == 0
    return pl.pallas_call(
        functools.partial(_kv_prep_kernel, blk=blk, hps=hps),
        grid=(m // tm, A_HEADS // hps),
        in_specs=[pl.BlockSpec((tm, KV_LORA), lambda i, h: (i, 0)),
                  pl.BlockSpec((hps, KV_LORA, A_NOPE), lambda i, h: (h, 0, 0)),
                  pl.BlockSpec((hps, A_VDIM, KV_LORA), lambda i, h: (h, 0, 0)),
                  pl.BlockSpec((tm, A_ROPE), lambda i, h: (i, 0))],
        out_specs=[pl.BlockSpec((hps, tm, A_NOPE + A_ROPE), lambda i, h: (h, i, 0)),
                   pl.BlockSpec((hps, tm // blk, 2 * A_VDIM, blk), lambda i, h: (h, i, 0, 0))],
        out_shape=[jax.ShapeDtypeStruct((A_HEADS, m, A_NOPE + A_ROPE), BF16),
                   jax.ShapeDtypeStruct((A_HEADS, m // blk, 2 * A_VDIM, blk), BF16)],
        compiler_params=_params(2),
        name="kv_prep",
    )(ckv_b, wk, wvt, krope)


def _softmax_step(s, v, m, l, acc):
    m_new = jnp.maximum(m, jnp.max(s, axis=1, keepdims=True))
    p = jnp.exp(s - m_new)
    alpha = jnp.exp(m - m_new)
    l = alpha * l + jnp.sum(p, axis=1, keepdims=True)
    acc = alpha * acc + _dot(p.astype(BF16), v)
    return m_new, l, acc


def _flash_kernel(q_ref, k_ref, vt_ref, o_ref, m_ref, acc_ref, *, blk, hps):
    qi = pl.program_id(2)
    m_ref[...] = jnp.full_like(m_ref, -jnp.inf)
    acc_ref[...] = jnp.zeros_like(acc_ref)

    def block(kj, diagonal):
        start = pl.multiple_of(kj * blk, blk)
        scores = [_dot_nt(k_ref[h, pl.ds(start, blk), :], q_ref[h]) for h in range(hps)]
        if diagonal:
            key = lax.broadcasted_iota(jnp.int32, (blk, blk), 0)
            qry = lax.broadcasted_iota(jnp.int32, (blk, blk), 1)
            scores = [jnp.where(key <= qry, s, -jnp.inf) for s in scores]
        probs, alphas = [], []
        for h, s in enumerate(scores):
            m_old = m_ref[h]
            m_new = jnp.maximum(m_old, jnp.max(s, axis=0, keepdims=True))
            probs.append(jnp.exp(s - m_new).astype(BF16))
            alphas.append(jnp.exp(m_old - m_new))
            m_ref[h] = m_new
        for h in range(hps):
            acc_ref[h] = alphas[h] * acc_ref[h] + _dot(vt_ref[h, kj], probs[h])

    def body(kj, carry):
        block(kj, False)
        return carry

    lax.fori_loop(0, qi, body, 0)
    block(qi, True)
    for h in range(hps):
        acc = acc_ref[h]
        out_t = acc[:A_VDIM] / acc[A_VDIM:A_VDIM + 1]
        o_ref[:, h * A_VDIM:(h + 1) * A_VDIM] = out_t.T.astype(o_ref.dtype)


def flash_prompt(q, k, vt, batch, seq, blk):
    nq = seq // blk
    dqk = q.shape[2]
    hps = ATT_HEADS_PER_STEP
    assert A_HEADS % hps == 0
    return pl.pallas_call(
        functools.partial(_flash_kernel, blk=blk, hps=hps),
        grid=(batch, A_HEADS // hps, nq),
        in_specs=[pl.BlockSpec((hps, blk, dqk), lambda b, h, i: (h, b * nq + i, 0)),
                  pl.BlockSpec((hps, seq, dqk), lambda b, h, i: (h, b, 0)),
                  pl.BlockSpec((hps, nq, 2 * A_VDIM, blk), lambda b, h, i: (h, b, 0, 0))],
        out_specs=pl.BlockSpec((blk, hps * A_VDIM), lambda b, h, i: (b * nq + i, h)),
        out_shape=jax.ShapeDtypeStruct((batch * seq, A_HEADS * A_VDIM), BF16),
        scratch_shapes=[pltpu.VMEM((hps, 1, blk), F32), pltpu.VMEM((hps, 2 * A_VDIM, blk), F32)],
        compiler_params=_params(3),
        name="flash_prompt",
    )(q, k, vt)


def _paged_kernel(pt_ref, q_ref, ckv_new_ref, kr_new_ref, pool_ckv, pool_kr, o_ref,
                  ckv_buf, kr_buf, sem, m_ref, l_ref, acc_ref, *, n_steps, dec_seq):
    pp = PAGES_PER_STEP
    g = pl.program_id(1)
    step = pl.program_id(0) * n_steps + g
    last_step = pl.num_programs(0) * n_steps - 1
    slot = step % 2

    def page_copies(s, slot_idx):
        copies = []
        for i in range(pp):
            page = pt_ref[s * pp + i]
            copies.append(pltpu.make_async_copy(pool_ckv.at[page], ckv_buf.at[slot_idx, i], sem.at[slot_idx, 0]))
            copies.append(pltpu.make_async_copy(pool_kr.at[page], kr_buf.at[slot_idx, i], sem.at[slot_idx, 1]))
        return copies

    @pl.when(step == 0)
    def _():
        for c in page_copies(0, 0):
            c.start()

    nxt = jnp.minimum(step + 1, last_step)
    for c in page_copies(nxt, 1 - slot):
        c.start()
    for c in page_copies(step, slot):
        c.wait()
    ckv_refs = [ckv_buf.at[slot, i] for i in range(pp)]
    kr_refs = [kr_buf.at[slot, i] for i in range(pp)]

    @pl.when(g == 0)
    def _():
        m_ref[...] = jnp.full_like(m_ref, -jnp.inf)
        l_ref[...] = jnp.zeros_like(l_ref)
        acc_ref[...] = jnp.zeros_like(acc_ref)

    q = q_ref[...]
    q_lat, q_rope = q[:, :KV_LORA], q[:, KV_LORA:]
    grp = PAGES_PER_GROUP
    n_chain = pp // grp
    kcs, scores, probs, alphas = {}, {}, {}, {}

    def score(c):
        lo = c * grp
        kcs[c] = jnp.concatenate([r[...].astype(BF16) for r in ckv_refs[lo:lo + grp]], axis=0)
        krt = jnp.concatenate([r[...].astype(BF16) for r in kr_refs[lo:lo + grp]], axis=1)
        scores[c] = _dot_nt(q_lat, kcs[c]) + _dot(q_rope, krt)

    def stats(c):
        m_old = m_ref[c]
        m_new = jnp.maximum(m_old, jnp.max(scores[c], axis=1, keepdims=True))
        p = jnp.exp(scores[c] - m_new)
        alphas[c] = jnp.exp(m_old - m_new)
        l_ref[c] = alphas[c] * l_ref[c] + jnp.sum(p, axis=1, keepdims=True)
        m_ref[c] = m_new
        probs[c] = p.astype(BF16)

    def value(c):
        acc_ref[c] = alphas[c] * acc_ref[c] + _dot(probs[c], kcs[c])

    for c in range(n_chain + 2):
        if c < n_chain:
            score(c)
        if 0 <= c - 1 < n_chain:
            stats(c - 1)
        if 0 <= c - 2 < n_chain:
            value(c - 2)

    @pl.when(g == n_steps - 1)
    def _():
        def pad_rows(x):
            return jnp.concatenate([x, jnp.zeros((PAGE_SIZE - dec_seq, x.shape[1]), x.dtype)], axis=0)

        kcn = pad_rows(ckv_new_ref[...]).astype(BF16)
        krn = pad_rows(kr_new_ref[...]).astype(BF16)
        sn = _dot_nt(q_lat, kcn) + _dot_nt(q_rope, krn)
        tok = lax.broadcasted_iota(jnp.int32, sn.shape, 0) % dec_seq
        key = lax.broadcasted_iota(jnp.int32, sn.shape, 1)
        sn = jnp.where(key <= tok, sn, -jnp.inf)
        m, l, acc = _softmax_step(sn, kcn, m_ref[0], l_ref[0], acc_ref[0])
        for c in range(1, pp // grp):
            m_c = m_ref[c]
            m_new = jnp.maximum(m, m_c)
            a, b = jnp.exp(m - m_new), jnp.exp(m_c - m_new)
            l = a * l + b * l_ref[c]
            acc = a * acc + b * acc_ref[c]
            m = m_new
        o_ref[...] = (acc / l).astype(o_ref.dtype)

    @pl.when(step == last_step)
    def _():
        for c in page_copies(last_step, 1 - slot):
            c.wait()


def paged_attention(q, pool_ckv, pool_kr, page_table, ckv_new, kr_new):
    nb, rows, dq = q.shape
    n_pages = page_table.shape[1]
    dec_seq = ckv_new.shape[1]
    pp = PAGES_PER_STEP
    assert n_pages % pp == 0 and pp % PAGES_PER_GROUP == 0
    n_steps = n_pages // pp
    page = pool_ckv.shape[1]
    assert page == PAGE_SIZE and pool_kr.shape[1:] == (A_ROPE, page)

    in_specs = [pl.BlockSpec((None, rows, dq), lambda b, g, pt: (b, 0, 0)),
                pl.BlockSpec((None, dec_seq, KV_LORA), lambda b, g, pt: (b, 0, 0)),
                pl.BlockSpec((None, dec_seq, A_ROPE), lambda b, g, pt: (b, 0, 0)),
                pl.BlockSpec(memory_space=pl.ANY),
                pl.BlockSpec(memory_space=pl.ANY)]
    n_chain = pp // PAGES_PER_GROUP
    grid_spec = pltpu.PrefetchScalarGridSpec(
        num_scalar_prefetch=1,
        grid=(nb, n_steps),
        in_specs=in_specs,
        out_specs=pl.BlockSpec((None, rows, KV_LORA), lambda b, g, pt: (b, 0, 0)),
        scratch_shapes=[pltpu.VMEM((2, pp, page, KV_LORA), F32),
                        pltpu.VMEM((2, pp, A_ROPE, page), F32),
                        pltpu.SemaphoreType.DMA((2, 2)),
                        pltpu.VMEM((n_chain, rows, 1), F32),
                        pltpu.VMEM((n_chain, rows, 1), F32),
                        pltpu.VMEM((n_chain, rows, KV_LORA), F32)],
    )
    return pl.pallas_call(
        functools.partial(_paged_kernel, n_steps=n_steps, dec_seq=dec_seq),
        grid_spec=grid_spec,
        out_shape=jax.ShapeDtypeStruct((nb, rows, KV_LORA), BF16),
        compiler_params=_params(2),
        name="paged_attention",
    )(page_table.reshape(-1), q, ckv_new, kr_new, pool_ckv, pool_kr)


def _head_mm_kernel(x_ref, w_ref, o_ref):
    o_ref[...] = _dot(x_ref[...], w_ref[...]).astype(o_ref.dtype)


def head_matmul(x, w):
    nh, m, kd = x.shape
    n = w.shape[2]
    return pl.pallas_call(
        _head_mm_kernel,
        grid=(nh,),
        in_specs=[pl.BlockSpec((None, m, kd), lambda h: (h, 0, 0)),
                  pl.BlockSpec((None, kd, n), lambda h: (h, 0, 0))],
        out_specs=pl.BlockSpec((m, n), lambda h: (0, h)),
        out_shape=jax.ShapeDtypeStruct((m, nh * n), BF16),
        compiler_params=_params(1),
        name="head_matmul",
    )(x, w)


def _mlstm_kernel(q_ref, k_ref, v_ref, if_ref, mo_ref, bias_ref, gmh_ref, c0_ref, n0_ref, m0_ref,
                  o_ref, c_ref, n_ref, m_ref, *, rows, lp, nseq):
    chunk = pl.program_id(1)

    @pl.when(chunk == 0)
    def _():
        c_ref[...] = c0_ref[...]
        n_ref[...] = n0_ref[...]
        m_ref[...] = m0_ref[...]

    def pad(x):
        if rows == lp:
            return x
        return jnp.concatenate([x, jnp.zeros((lp - rows, x.shape[1]), x.dtype)], axis=0)

    t_idx = lax.broadcasted_iota(jnp.int32, (lp, lp), 0)
    s_idx = lax.broadcasted_iota(jnp.int32, (lp, lp), 1)
    causal = s_idx <= t_idx
    diag = s_idx == t_idx
    valid = lax.broadcasted_iota(jnp.int32, (lp, 1), 0) < rows
    for sq, h in [(sq, h) for sq in range(nseq) for h in range(M_HEADS)]:
        r0 = sq * rows
        gates = pad(if_ref[r0:r0 + rows, :] + bias_ref[...])
        q = pad(q_ref[r0:r0 + rows, h * M_DQK:(h + 1) * M_DQK]) * (M_DQK ** -0.5)
        k = pad(k_ref[r0:r0 + rows, h * M_DQK:(h + 1) * M_DQK])
        v = pad(v_ref[r0:r0 + rows, h * M_DV:(h + 1) * M_DV])
        i_col = gates[:, h:h + 1]
        f_pre = gates[:, M_HEADS + h:M_HEADS + h + 1]
        f_col = jnp.minimum(f_pre, 0.0) - jnp.log1p(jnp.exp(-jnp.abs(f_pre)))
        if rows != lp:
            i_col = jnp.where(valid, i_col, -jnp.inf)
            f_col = jnp.where(valid, f_col, 0.0)

        f_row = jnp.sum(jnp.where(diag, f_col, 0.0), axis=0, keepdims=True)
        i_row = jnp.sum(jnp.where(diag, i_col, 0.0), axis=0, keepdims=True)
        b_col = jnp.sum(jnp.where(causal, f_row, 0.0), axis=1, keepdims=True)
        b_row = jnp.sum(jnp.where(t_idx <= s_idx, f_col, 0.0), axis=0, keepdims=True)

        m_prev = m_ref[sq, h]
        c_prev = c_ref[sq, h]
        n_prev = n_ref[sq, h]
        a_col = b_col + m_prev
        dmat = jnp.where(causal, b_col - b_row + i_row, -jnp.inf)
        mt = jnp.maximum(a_col, jnp.max(dmat, axis=1, keepdims=True))
        w_inter = jnp.exp(a_col - mt)
        qb, kb, vb = q.astype(BF16), k.astype(BF16), v.astype(BF16)
        qk = _dot_nt(qb, kb) * jnp.exp(dmat - mt)
        num = _dot(qk.astype(BF16), vb) + w_inter * _dot(qb, c_prev.astype(BF16))
        den = jnp.sum(qk, axis=1, keepdims=True) + w_inter * jnp.sum(q * n_prev, axis=1, keepdims=True)
        hid = num / jnp.maximum(jnp.abs(den), jnp.exp(-mt))

        b_last = b_col[lp - 1:lp, :]
        g_col = b_last - b_col + i_col
        m_new = jnp.maximum(b_last + m_prev, jnp.max(g_col, axis=0, keepdims=True))
        decay = jnp.exp(b_last + m_prev - m_new)
        kw = jnp.exp(g_col - m_new) * k
        c_ref[sq, h] = decay * c_prev + _dot_tn(kw.astype(BF16), vb)
        n_ref[sq, h] = decay * n_prev + jnp.sum(kw, axis=0, keepdims=True)
        m_ref[sq, h] = m_new

        hn = hid * lax.rsqrt(jnp.mean(hid * hid, axis=1, keepdims=True) + EPS) * gmh_ref[h]
        out = jax.nn.sigmoid(pad(mo_ref[r0:r0 + rows, h * M_DV:(h + 1) * M_DV])) * hn
        o_ref[r0:r0 + rows, h * M_DV:(h + 1) * M_DV] = out[:rows].astype(o_ref.dtype)


def mlstm(proj, lay, bias_row, g_mh, c0, n0, m0, batch, seq):
    rows = _tile(seq, MLSTM_CHUNK, 8)
    lp = max(rows, MLSTM_MIN_ROWS)
    nc = seq // rows
    nseq = MLSTM_SEQS_PER_STEP if (nc == 1 and batch % MLSTM_SEQS_PER_STEP == 0) else 1
    tr = nseq * rows
    hq, hv = M_HEADS * M_DQK, M_HEADS * M_DV
    oq, ok, ov, oi, oo = (lay[n][0] for n in ("mq", "mk", "mv", "mif", "mo"))
    assert oq % hq == 0 and ok % hq == 0 and ov % hv == 0 and oo % hv == 0 and oi % LANES == 0

    def rowmap(col_block):
        return lambda b, c: (b * nc + c, col_block)

    def state_spec(*dims):
        return pl.BlockSpec((nseq, M_HEADS) + dims, lambda b, c: (b, 0, 0, 0))

    out, c1, n1, m1 = pl.pallas_call(
        functools.partial(_mlstm_kernel, rows=rows, lp=lp, nseq=nseq),
        grid=(batch // nseq, nc),
        in_specs=[pl.BlockSpec((tr, hq), rowmap(oq // hq)),
                  pl.BlockSpec((tr, hq), rowmap(ok // hq)),
                  pl.BlockSpec((tr, hv), rowmap(ov // hv)),
                  pl.BlockSpec((tr, LANES), rowmap(oi // LANES)),
                  pl.BlockSpec((tr, hv), rowmap(oo // hv)),
                  pl.BlockSpec((1, LANES), lambda b, c: (0, 0)),
                  pl.BlockSpec((M_HEADS, 1, M_DV), lambda b, c: (0, 0, 0)),
                  state_spec(M_DQK, M_DV), state_spec(1, M_DQK), state_spec(1, 1)],
        out_specs=[pl.BlockSpec((tr, hv), lambda b, c: (b * nc + c, 0)),
                   state_spec(M_DQK, M_DV), state_spec(1, M_DQK), state_spec(1, 1)],
        out_shape=[jax.ShapeDtypeStruct((batch * seq, hv), BF16),
                   jax.ShapeDtypeStruct((batch, M_HEADS, M_DQK, M_DV), F32),
                   jax.ShapeDtypeStruct((batch, M_HEADS, 1, M_DQK), F32),
                   jax.ShapeDtypeStruct((batch, M_HEADS, 1, 1), F32)],
        compiler_params=_params(2),
        name="mlstm",
    )(proj, proj, proj, proj, proj, bias_row, g_mh.reshape(M_HEADS, 1, M_DV),
      c0, n0.reshape(batch, M_HEADS, 1, M_DQK), m0.reshape(batch, M_HEADS, 1, 1))
    return out, c1, n1.reshape(batch, M_HEADS, M_DQK), m1.reshape(batch, M_HEADS)


def _ffn_up_kernel(x_ref, wg_ref, wv_ref, c0_ref, w_ref, b_ref, act_ref, tail_ref, wg_bf, wv_bf, *carry,
                   nb, tiles_per_seq):
    i = pl.program_id(1)
    _cast_weights_once([wg_ref, wv_ref], [wg_bf, wv_bf])
    x = x_ref[...]
    tm, tc = act_ref.shape
    g = _dot(x, wg_bf[...]).reshape(nb, tm // nb, tc)
    val = _dot(x, wv_bf[...]).reshape(nb, tm // nb, tc)
    st = tm // nb
    last_rows = g[:, st - (CONV_W - 1):, :]
    prev = c0_ref[...]
    if tiles_per_seq > 1:
        carry_ref = carry[0]

        @pl.when(i == 0)
        def _():
            carry_ref[...] = jnp.zeros_like(carry_ref)

        prev = jnp.where(i % tiles_per_seq == 0, prev, carry_ref[...])
        carry_ref[...] = last_rows
    t = lax.broadcasted_iota(jnp.int32, g.shape, 1)
    prev1 = jnp.where(t == 0, prev[:, 1:2, :], pltpu.roll(g, 1, 1))
    prev2 = jnp.where(t == 0, prev[:, 0:1, :], jnp.where(t == 1, prev[:, 1:2, :], pltpu.roll(g, 2, 1)))
    w = w_ref[...]
    conv = b_ref[...] + w[0:1, :] * prev2 + w[1:2, :] * prev1 + w[2:3, :] * g
    act_ref[...] = (jax.nn.gelu(conv, approximate=True) * val).reshape(tm, tc).astype(act_ref.dtype)
    tail_ref[...] = last_rows


def ffn_up_convglu(x, w_up, conv0, w_conv, b_conv, batch, seq):
    assert CONV_W == 3 and seq >= CONV_W - 1
    m, kd = x.shape
    tc = _tile(D_FF, 256)
    ncol = D_FF // tc
    if seq >= ROW_TILE:
        tm, nb = _tile(seq, ROW_TILE, 8), 1
    else:
        nb = max(1, min(batch, ROW_TILE // seq))
        assert batch % nb == 0
        tm = nb * seq
    tps = seq // (tm // nb)
    assert w_up.dtype == F32
    scratch = [pltpu.VMEM((kd, tc), BF16), pltpu.VMEM((kd, tc), BF16)]
    if tps > 1:
        scratch.append(pltpu.VMEM((nb, CONV_W - 1, tc), F32))
    act, tail = pl.pallas_call(
        functools.partial(_ffn_up_kernel, nb=nb, tiles_per_seq=tps),
        grid=(ncol, m // tm),
        in_specs=[pl.BlockSpec((tm, kd), lambda j, i: (i, 0)),
                  pl.BlockSpec((kd, tc), lambda j, i: (0, j)),
                  pl.BlockSpec((kd, tc), lambda j, i: (0, j + ncol)),
                  pl.BlockSpec((nb, CONV_W - 1, tc), lambda j, i: (i // tps, 0, j)),
                  pl.BlockSpec((CONV_W, tc), lambda j, i: (0, j)),
                  pl.BlockSpec((1, tc), lambda j, i: (0, j))],
        out_specs=[pl.BlockSpec((tm, tc), lambda j, i: (i, j)),
                   pl.BlockSpec((nb, CONV_W - 1, tc), lambda j, i: (i, 0, j))],
        out_shape=[jax.ShapeDtypeStruct((m, D_FF), BF16),
                   jax.ShapeDtypeStruct((batch * tps, CONV_W - 1, D_FF), F32)],
        scratch_shapes=scratch,
        compiler_params=_params(2),
        name="ffn_up_convglu",
    )(x, w_up, w_up, conv0, w_conv, b_conv.reshape(1, D_FF))
    return act, tail.reshape(batch, tps, CONV_W - 1, D_FF)[:, tps - 1]


_MODE_COPY, _MODE_DUP_ROPE, _MODE_GATES, _MODE_ZERO = 0, 1, 2, 3


def _relayout_kernel(off_ref, mode_ref, src_ref, o_ref):
    mode = mode_ref[pl.program_id(0)]
    row = lax.broadcasted_iota(jnp.int32, src_ref.shape, 0)

    @pl.when(mode == _MODE_COPY)
    def _():
        o_ref[...] = src_ref[...].T.astype(o_ref.dtype)

    @pl.when(mode == _MODE_DUP_ROPE)
    def _():
        kr = src_ref[:A_ROPE, :]
        o_ref[...] = jnp.concatenate([kr, kr], axis=0).T.astype(o_ref.dtype)

    @pl.when(mode == _MODE_GATES)
    def _():
        o_ref[...] = jnp.where(row < 2 * M_HEADS, src_ref[...], 0.0).T.astype(o_ref.dtype)

    @pl.when(mode == _MODE_ZERO)
    def _():
        o_ref[...] = jnp.zeros_like(o_ref)


def relayout_w_in(w_in_t, lay, total, src_off):
    assert 2 * A_ROPE == LANES and 2 * M_HEADS <= LANES
    n_tiles = total // LANES
    off, mode = np.zeros(n_tiles, np.int32), np.full(n_tiles, _MODE_ZERO, np.int32)
    groups = [(lay[n][0], lay[n][1], src_off[n], _MODE_COPY) for n in ("cq", "mq", "mk", "ckv", "mv", "mo", "ga", "gb")]
    groups += [(*lay["kr2"], src_off["kr"], _MODE_DUP_ROPE), (*lay["mif"], src_off["mi"], _MODE_GATES)]
    n_src, rows = w_in_t.shape
    for dst, width, s0, md in groups:
        assert dst % LANES == 0 and width % LANES == 0 and s0 % 8 == 0
        for t in range(width // LANES):
            off[dst // LANES + t], mode[dst // LANES + t] = s0 + t * LANES, md
    assert int(off.max()) + LANES <= n_src
    return pl.pallas_call(
        _relayout_kernel,
        grid_spec=pltpu.PrefetchScalarGridSpec(
            num_scalar_prefetch=2,
            grid=(n_tiles,),
            in_specs=[pl.BlockSpec((pl.Element(LANES), pl.Element(rows)),
                                   lambda t, off8, mode: (pl.multiple_of(off8[t] * 8, 8), 0))],
            out_specs=pl.BlockSpec((rows, LANES), lambda t, off, mode: (0, t)),
        ),
        out_shape=jax.ShapeDtypeStruct((rows, total), BF16),
        compiler_params=_params(1),
        name="relayout_w_in",
    )(jnp.asarray(off // 8), jnp.asarray(mode), w_in_t)


def _prepare_weights(w):
    lay, total = _padded_layout()
    offs = np.concatenate([[0], np.cumsum([Q_LORA, KV_LORA, A_ROPE, M_HEADS * M_DQK, M_HEADS * M_DQK,
                                            M_HEADS * M_DV, M_HEADS, M_HEADS, M_HEADS * M_DV, D_MODEL, D_MODEL])])
    src = {n: (int(offs[i]), int(offs[i + 1])) for i, n in enumerate(
        ["cq", "ckv", "kr", "mq", "mk", "mv", "mi", "mf", "mo", "ga", "gb"])}
    assert src["mf"][0] == src["mi"][1]
    w_in_p = relayout_w_in(jnp.swapaxes(w["w_in"], 0, 1), lay, total, {n: s[0] for n, s in src.items()})

    w_uq = w["w_uq"]
    rope_cols = w_uq[..., A_NOPE:]
    wq = jnp.concatenate([w_uq[..., :A_NOPE], rope_cols, rope_cols], axis=-1).transpose(1, 0, 2).astype(BF16)
    bias_row = jnp.concatenate([w["b_i"].astype(F32), w["b_f"].astype(F32),
                                jnp.zeros((LANES - 2 * M_HEADS,), F32)]).reshape(1, LANES)
    return {
        "lay": lay, "n_proj": total, "w_in_p": w_in_p, "wq": wq,
        "wuk_t": w["w_uk"].transpose(1, 2, 0).astype(BF16),
        "wk": w["w_uk"].transpose(1, 0, 2).astype(BF16),
        "wv": w["w_uv"].transpose(1, 0, 2).astype(BF16),
        "wvt": w["w_uv"].transpose(1, 2, 0).astype(BF16),
        "bias_row": bias_row,
        "w_br_a": w["w_br_a"].astype(F32), "w_br_b": w["w_br_b"].astype(F32),
        "w_o": w["w_o"].astype(F32), "w_up": w["w_up"].astype(F32), "w_down": w["w_down"].astype(BF16),
        "w_ple_gate": w["w_ple_gate"].astype(F32), "w_ple_proj": w["w_ple_proj"].astype(F32),
    }


def _rope_tables(pos, batch):
    half = A_ROPE // 2
    inv = jnp.power(ROPE_THETA, -jnp.arange(half, dtype=F32) / half)
    ang = pos[:, None] * inv[None, :]
    cos, sin = jnp.cos(ang), jnp.sin(ang)
    reps = LANES // A_ROPE
    cos_t = jnp.tile(jnp.concatenate([cos, cos], axis=1), (batch, reps))
    sin_t = jnp.tile(jnp.concatenate([-sin, sin], axis=1), (batch, reps))
    return cos_t, sin_t


def _hybrid_layer(x, pe, pos, w, pw, c0, n0, m0, conv0, paged):
    batch, seq, _ = x.shape
    m = batch * seq
    lay = pw["lay"]
    x2 = x.reshape(m, D_MODEL)
    cos, sin = _rope_tables(pos, batch)

    h = rmsnorm_rows(x2, w["g_mix"], BF16)
    proj = fused_matmul([(h, pw["w_in_p"])], [], _ep_plain, pw["n_proj"], F32,
                        tn=_tile(pw["n_proj"], 1024), name="in_proj")
    cqn, c_kv, ckv_b, k_rope = mla_prep(proj, lay, w["g_qa"], w["g_kva"], cos, sin)

    if paged is None:
        q = q_prep(cqn, pw["wq"], cos, sin)
        blk = _tile(seq, ATT_BLOCK)
        k, vt = kv_prep(ckv_b, k_rope, pw["wk"], pw["wvt"], blk)
        a_out = flash_prompt(q, k, vt, batch, seq, blk)
    else:
        pool_ckv, pool_kr, page_table = paged
        q = q_prep(cqn, pw["wq"], cos, sin, pw["wuk_t"])
        dq = q.shape[2]
        q = q.reshape(A_HEADS, batch, seq, dq).transpose(1, 0, 2, 3).reshape(batch, A_HEADS * seq, dq)
        o_lat = paged_attention(q, pool_ckv, pool_kr, page_table,
                                c_kv.reshape(batch, seq, KV_LORA), k_rope.reshape(batch, seq, A_ROPE))
        o_lat = o_lat.reshape(batch, A_HEADS, seq, KV_LORA).transpose(1, 0, 2, 3).reshape(A_HEADS, m, KV_LORA)
        a_out = head_matmul(o_lat, pw["wv"])

    b_out, c1, n1, m1 = mlstm(proj, lay, pw["bias_row"], w["g_mh"], c0, n0, m0, batch, seq)

    merged = fused_matmul([(a_out, pw["w_br_a"]), (b_out, pw["w_br_b"])],
                          [(proj, lay["ga"][0]), (proj, lay["gb"][0])],
                          _ep_gated_merge, D_MODEL, BF16, name="branch_merge")
    x2 = fused_matmul([(merged, pw["w_o"])], [(x2, 0)], _ep_residual, D_MODEL, F32, name="out_proj")

    hf = rmsnorm_rows(x2, w["g_ffn"], BF16)
    act, conv_tail = ffn_up_convglu(hf, pw["w_up"], conv0.astype(F32), w["w_conv"], w["b_conv"], batch, seq)
    x2 = fused_matmul([(act, pw["w_down"])], [(x2, 0)], _ep_residual, D_MODEL, F32,
                      tn=_tile(D_MODEL, 512), tk=_tile(D_FF, D_FF // 2), name="ffn_down")

    hp = rmsnorm_rows(x2, w["g_ple"], BF16)
    x2 = fused_matmul([(hp, pw["w_ple_gate"]), (pe.reshape(m, PLE_DIM).astype(BF16), pw["w_ple_proj"])],
                      [(x2, 0)], _ep_ple, D_MODEL, F32, name="ple")
    return (x2, c_kv.reshape(batch, seq, KV_LORA), k_rope.reshape(batch, seq, A_ROPE), c1, n1, m1, conv_tail)


def kernel(x_prompt, x_sample, cache_ckv, cache_krope, state_C, state_n, state_m, state_conv, page_table,
           p_prompt, p_sample, g_mix, w_in, b_i, b_f, g_qa, w_uq, g_kva, w_uk, w_uv, g_mh, w_br_a, w_br_b,
           w_o, g_ffn, w_up, w_conv, b_conv, w_down, g_ple, w_ple_gate, w_ple_proj, g_final):
    bp, sp = x_prompt.shape[0], x_prompt.shape[1]
    bs, ss = x_sample.shape[0], x_sample.shape[1]
    depth = w_in.shape[0]
    past = page_table.shape[1] * cache_ckv.shape[2]
    pos_p = jnp.arange(sp, dtype=F32)
    pos_s = past + jnp.arange(ss, dtype=F32)
    c0p = jnp.zeros((bp, M_HEADS, M_DQK, M_DV), F32)
    n0p = jnp.zeros((bp, M_HEADS, M_DQK), F32)
    m0p = jnp.zeros((bp, M_HEADS), F32)
    conv0p = jnp.zeros((bp, CONV_W - 1, D_FF), F32)
    names = ["g_mix", "w_in", "b_i", "b_f", "g_qa", "w_uq", "g_kva", "w_uk", "w_uv", "g_mh", "w_br_a", "w_br_b",
             "w_o", "g_ffn", "w_up", "w_conv", "b_conv", "w_down", "g_ple", "w_ple_gate", "w_ple_proj"]
    stacked = [g_mix, w_in, b_i, b_f, g_qa, w_uq, g_kva, w_uk, w_uv, g_mh, w_br_a, w_br_b,
               w_o, g_ffn, w_up, w_conv, b_conv, w_down, g_ple, w_ple_gate, w_ple_proj]
    xp, xs = x_prompt, x_sample
    outs_p = [[] for _ in range(6)]
    outs_s = [[] for _ in range(6)]
    for l in range(depth):
        w = {n: a[l] for n, a in zip(names, stacked)}
        pw = _prepare_weights(w)
        xp2, *new_p = _hybrid_layer(xp, p_prompt[l], pos_p, w, pw, c0p, n0p, m0p, conv0p, None)
        xs2, *new_s = _hybrid_layer(xs, p_sample[l], pos_s, w, pw, state_C[l], state_n[l], state_m[l],
                                    state_conv[l],
                                    (cache_ckv[l], jnp.swapaxes(cache_krope[l], 1, 2), page_table))
        xp, xs = xp2.reshape(xp.shape), xs2.reshape(xs.shape)
        for j in range(6):
            outs_p[j].append(new_p[j])
            outs_s[j].append(new_s[j])
    stack_p = [jnp.stack(o) for o in outs_p]
    stack_s = [jnp.stack(o) for o in outs_s]
    y_prompt = rmsnorm_rows(xp.reshape(bp * sp, D_MODEL), g_final, F32).reshape(xp.shape)
    y_sample = rmsnorm_rows(xs.reshape(bs * ss, D_MODEL), g_final, F32).reshape(xs.shape)
    return (y_prompt, y_sample, *stack_p, *stack_s)
```

```python
import functools

import jax
import jax.numpy as jnp
import numpy as np
from jax import lax
from jax.experimental import pallas as pl
from jax.experimental.pallas import tpu as pltpu

D_MODEL = 4096
DEPTH = 1
PAGE_SIZE = 128
A_HEADS = 16
A_NOPE = 128
A_ROPE = 64
A_VDIM = 128
Q_LORA = 1024
KV_LORA = 512
ROPE_THETA = 10000.0
A_SCALE = (A_NOPE + A_ROPE) ** -0.5
M_HEADS = 4
M_DQK = 256
M_DV = 512
D_FF = 11008
CONV_W = 3
PLE_DIM = 256
EPS = 1e-6

LANES = 128
VMEM_LIMIT = 56 * 1024 * 1024
ROW_TILE = 1024
COL_TILE = 512
EPILOGUE_SUBTILES = 4
MLSTM_CHUNK = 256
MLSTM_MIN_ROWS = 128
MLSTM_SEQS_PER_STEP = 4
ATT_BLOCK = 256
ATT_HEADS_PER_STEP = 8
PREP_HEADS_PER_STEP = 4
PAGES_PER_STEP = 64
PAGES_PER_GROUP = 16

F32 = jnp.float32
BF16 = jnp.bfloat16


def _params(n_grid):
    return pltpu.CompilerParams(dimension_semantics=("arbitrary",) * n_grid,
                                vmem_limit_bytes=VMEM_LIMIT)


def _tile(n, pref, unit=LANES):
    if n <= pref:
        return n
    best = None
    for t in range(unit, pref + 1, unit):
        if n % t == 0:
            best = t
    assert best is not None, (n, pref)
    return best


def _padded_layout():
    hq, hv = M_HEADS * M_DQK, M_HEADS * M_DV
    widths = [("cq", Q_LORA), ("mq", hq), ("mk", hq), ("ckv", KV_LORA), ("kr2", 2 * A_ROPE), ("mif", LANES)]
    off, lay = 0, {}
    for name, w in widths:
        lay[name] = (off, w)
        off += w
    off = -(-off // hv) * hv
    for name, w in [("mv", hv), ("mo", hv), ("ga", D_MODEL), ("gb", D_MODEL)]:
        lay[name] = (off, w)
        off += w
    return lay, off


def _dot(a, b):
    return jnp.dot(a, b, preferred_element_type=F32)


def _dot_nt(a, b):
    return lax.dot_general(a, b, (((1,), (1,)), ((), ())), preferred_element_type=F32)


def _dot_tn(a, b):
    return lax.dot_general(a, b, (((0,), (0,)), ((), ())), preferred_element_type=F32)


def _rmsnorm_kernel(x_ref, g_ref, o_ref):
    x = x_ref[...]
    y = x * lax.rsqrt(jnp.mean(x * x, axis=-1, keepdims=True) + EPS)
    o_ref[...] = (y * g_ref[...]).astype(o_ref.dtype)


def rmsnorm_rows(x, g, out_dtype):
    m, d = x.shape
    tm = _tile(m, 512, 8)
    return pl.pallas_call(
        _rmsnorm_kernel,
        grid=(m // tm,),
        in_specs=[pl.BlockSpec((tm, d), lambda i: (i, 0)),
                  pl.BlockSpec((1, d), lambda i: (0, 0))],
        out_specs=pl.BlockSpec((tm, d), lambda i: (i, 0)),
        out_shape=jax.ShapeDtypeStruct((m, d), out_dtype),
        compiler_params=_params(1),
        name="rmsnorm",
    )(x, g.reshape(1, d).astype(F32))


def _cast_weights_once(w_refs, cast_refs):
    @pl.when(pl.program_id(1) == 0)
    def _():
        for w, c in zip(w_refs, cast_refs):
            c[...] = w[...].astype(c.dtype)


def _matmul_kernel(*refs, n_pairs, n_extra, epilogue, nk, n_cast):
    pairs = [(refs[2 * p], refs[2 * p + 1]) for p in range(n_pairs)]
    extras = refs[2 * n_pairs:2 * n_pairs + n_extra]
    o_ref = refs[2 * n_pairs + n_extra]
    if nk == 1:
        if n_cast:
            cast_refs = refs[2 * n_pairs + n_extra + 1:]
            _cast_weights_once([w for _, w in pairs], cast_refs)
            pairs = [(x, c) for (x, _), c in zip(pairs, cast_refs)]
        tm = o_ref.shape[0]
        nsub = EPILOGUE_SUBTILES if (n_extra and tm % (8 * EPILOGUE_SUBTILES) == 0) else 1
        rs = tm // nsub
        accs = [[_dot(x[s * rs:(s + 1) * rs, :], w[...]) for x, w in pairs] for s in range(nsub)]
        for s in range(nsub):
            rows = slice(s * rs, (s + 1) * rs)
            o_ref[rows, :] = epilogue(accs[s], [e[rows, :] for e in extras]).astype(o_ref.dtype)
        return
    acc_ref = refs[2 * n_pairs + n_extra + 1]
    k = pl.program_id(2)
    x, w = pairs[0]

    @pl.when(k == 0)
    def _():
        acc_ref[...] = jnp.zeros_like(acc_ref)

    acc_ref[...] += _dot(x[...], w[...])

    @pl.when(k == nk - 1)
    def _():
        o_ref[...] = epilogue([acc_ref[...]], [e[...] for e in extras]).astype(o_ref.dtype)


def fused_matmul(pairs, extras, epilogue, n_out, out_dtype, *, tm=None, tn=None, tk=None, name="matmul"):
    m = pairs[0][0].shape[0]
    tm = tm or _tile(m, ROW_TILE, 8)
    tn = tn or _tile(n_out, COL_TILE)
    kdim = pairs[0][0].shape[1]
    nk = 1 if tk is None else kdim // tk
    assert nk == 1 or len(pairs) == 1
    cast = any(w.dtype != BF16 for _, w in pairs)
    assert not cast or (nk == 1 and all(w.dtype == F32 for _, w in pairs))

    def order(f):
        return (lambda j, i: f(i, j)) if cast else f

    in_specs, args = [], []
    for x, w in pairs:
        kd = x.shape[1]
        if nk == 1:
            in_specs += [pl.BlockSpec((tm, kd), order(lambda i, j: (i, 0))),
                         pl.BlockSpec((kd, tn), order(lambda i, j: (0, j)))]
        else:
            in_specs += [pl.BlockSpec((tm, tk), lambda i, j, k: (i, k)),
                         pl.BlockSpec((tk, tn), lambda i, j, k: (k, j))]
        args += [x, w]
    for arr, off in extras:
        assert off % tn == 0
        ob = off // tn
        if nk == 1:
            in_specs.append(pl.BlockSpec((tm, tn), order(lambda i, j, ob=ob: (i, j + ob))))
        else:
            in_specs.append(pl.BlockSpec((tm, tn), lambda i, j, k, ob=ob: (i, j + ob)))
        args.append(arr)
    if nk == 1:
        grid = (n_out // tn, m // tm) if cast else (m // tm, n_out // tn)
        out_spec = pl.BlockSpec((tm, tn), order(lambda i, j: (i, j)))
        scratch = [pltpu.VMEM((x.shape[1], tn), BF16) for x, _ in pairs] if cast else []
    else:
        grid = (m // tm, n_out // tn, nk)
        out_spec = pl.BlockSpec((tm, tn), lambda i, j, k: (i, j))
        scratch = [pltpu.VMEM((tm, tn), F32)]
    return pl.pallas_call(
        functools.partial(_matmul_kernel, n_pairs=len(pairs), n_extra=len(extras), epilogue=epilogue, nk=nk,
                          n_cast=len(pairs) if cast else 0),
        grid=grid,
        in_specs=in_specs,
        out_specs=out_spec,
        out_shape=jax.ShapeDtypeStruct((m, n_out), out_dtype),
        scratch_shapes=scratch,
        compiler_params=_params(len(grid)),
        name=name,
    )(*args)


def _ep_plain(accs, extras):
    return accs[0]


def _ep_residual(accs, extras):
    return extras[0] + accs[0]


def _ep_gated_merge(accs, extras):
    return jax.nn.sigmoid(extras[0]) * accs[0] + jax.nn.sigmoid(extras[1]) * accs[1]


def _ep_ple(accs, extras):
    return extras[0] + jax.nn.sigmoid(accs[0]) * accs[1]


def _rope_mix(x, cos, sin):
    return x * cos + pltpu.roll(x, A_ROPE // 2, 1) * sin


def _mla_prep_kernel(cq_ref, ckv_ref, kr_ref, gq_ref, gkv_ref, cos_ref, sin_ref,
                     cqn_ref, ckv_out_ref, ckvb_ref, krope_ref):
    cq = cq_ref[...]
    cqn = cq * lax.rsqrt(jnp.mean(cq * cq, axis=-1, keepdims=True) + EPS) * gq_ref[...]
    cqn_ref[...] = cqn.astype(cqn_ref.dtype)
    ckv = ckv_ref[...]
    ckvn = ckv * lax.rsqrt(jnp.mean(ckv * ckv, axis=-1, keepdims=True) + EPS) * gkv_ref[...]
    ckv_out_ref[...] = ckvn
    ckvb_ref[...] = ckvn.astype(ckvb_ref.dtype)
    krope_ref[...] = _rope_mix(kr_ref[...], cos_ref[...], sin_ref[...])[:, :A_ROPE]


def mla_prep(proj, lay, g_qa, g_kva, cos, sin):
    m = proj.shape[0]
    tm = _tile(m, 512, 8)
    (o_cq, w_cq), (o_ckv, w_ckv), (o_kr, w_kr) = lay["cq"], lay["ckv"], lay["kr2"]
    assert o_cq % w_cq == 0 and o_ckv % w_ckv == 0 and o_kr % w_kr == 0
    return pl.pallas_call(
        _mla_prep_kernel,
        grid=(m // tm,),
        in_specs=[pl.BlockSpec((tm, w_cq), lambda i: (i, o_cq // w_cq)),
                  pl.BlockSpec((tm, w_ckv), lambda i: (i, o_ckv // w_ckv)),
                  pl.BlockSpec((tm, w_kr), lambda i: (i, o_kr // w_kr)),
                  pl.BlockSpec((1, w_cq), lambda i: (0, 0)),
                  pl.BlockSpec((1, w_ckv), lambda i: (0, 0)),
                  pl.BlockSpec((tm, LANES), lambda i: (i, 0)),
                  pl.BlockSpec((tm, LANES), lambda i: (i, 0))],
        out_specs=[pl.BlockSpec((tm, w_cq), lambda i: (i, 0)),
                   pl.BlockSpec((tm, w_ckv), lambda i: (i, 0)),
                   pl.BlockSpec((tm, w_ckv), lambda i: (i, 0)),
                   pl.BlockSpec((tm, A_ROPE), lambda i: (i, 0))],
        out_shape=[jax.ShapeDtypeStruct((m, w_cq), BF16),
                   jax.ShapeDtypeStruct((m, w_ckv), F32),
                   jax.ShapeDtypeStruct((m, w_ckv), BF16),
                   jax.ShapeDtypeStruct((m, A_ROPE), F32)],
        compiler_params=_params(1),
        name="mla_prep",
    )(proj, proj, proj, g_qa.reshape(1, -1), g_kva.reshape(1, -1), cos, sin)


def _q_prep_kernel(x_ref, w_ref, cos_ref, sin_ref, *rest, absorbed, hps):
    x, cos, sin = x_ref[...], cos_ref[...], sin_ref[...]
    for h in range(hps):
        acc = _dot(x, w_ref[h])
        nope = (acc[:, :A_NOPE] * A_SCALE).astype(BF16)
        rot = _rope_mix(acc[:, A_NOPE:], cos, sin)[:, :A_ROPE] * A_SCALE
        if absorbed:
            wuk_ref, o_ref = rest
            o_ref[h, :, :KV_LORA] = _dot(nope, wuk_ref[h]).astype(o_ref.dtype)
            o_ref[h, :, KV_LORA:] = rot.astype(o_ref.dtype)
        else:
            (o_ref,) = rest
            o_ref[h, :, :A_NOPE] = nope
            o_ref[h, :, A_NOPE:] = rot.astype(o_ref.dtype)


def q_prep(cqn, wq, cos, sin, wuk_t=None):
    m, kq = cqn.shape
    tm = _tile(m, ROW_TILE, 8)
    wcols = wq.shape[2]
    hps = PREP_HEADS_PER_STEP
    assert A_HEADS % hps == 0
    absorbed = wuk_t is not None
    width = (KV_LORA if absorbed else A_NOPE) + A_ROPE
    in_specs = [pl.BlockSpec((tm, kq), lambda i, h: (i, 0)),
                pl.BlockSpec((hps, kq, wcols), lambda i, h: (h, 0, 0)),
                pl.BlockSpec((tm, LANES), lambda i, h: (i, 0)),
                pl.BlockSpec((tm, LANES), lambda i, h: (i, 0))]
    args = [cqn, wq, cos, sin]
    if absorbed:
        in_specs.append(pl.BlockSpec((hps, A_NOPE, KV_LORA), lambda i, h: (h, 0, 0)))
        args.append(wuk_t)
    return pl.pallas_call(
        functools.partial(_q_prep_kernel, absorbed=absorbed, hps=hps),
        grid=(m // tm, A_HEADS // hps),
        in_specs=in_specs,
        out_specs=pl.BlockSpec((hps, tm, width), lambda i, h: (h, i, 0)),
        out_shape=jax.ShapeDtypeStruct((A_HEADS, m, width), BF16),
        compiler_params=_params(2),
        name="q_prep",
    )(*args)


def _kv_prep_kernel(x_ref, wk_ref, wvt_ref, kr_ref, k_ref, vt_ref, *, blk, hps):
    x = x_ref[...]
    kr = kr_ref[...].astype(k_ref.dtype)
    for h in range(hps):
        k_ref[h, :, :A_NOPE] = _dot(x, wk_ref[h]).astype(k_ref.dtype)
        k_ref[h, :, A_NOPE:] = kr
        vt = _dot_nt(wvt_ref[h], x).astype(vt_ref.dtype)
        for c in range(x.shape[0] // blk):
            vt_ref[h, c, :A_VDIM, :] = vt[:, c * blk:(c + 1) * blk]
            vt_ref[h, c, A_VDIM:, :] = jnp.ones((A_VDIM, blk), vt_ref.dtype)


def kv_prep(ckv_b, krope, wk, wvt, blk):
    m = ckv_b.shape[0]
    tm = _tile(m, ROW_TILE, blk)
    hps = PREP_HEADS_PER_STEP
    assert A_HEADS % hps == 0
    return pl.pallas_call(
        functools.partial(_kv_prep_kernel, blk=blk, hps=hps),
        grid=(m // tm, A_HEADS // hps),
        in_specs=[pl.BlockSpec((tm, KV_LORA), lambda i, h: (i, 0)),
                  pl.BlockSpec((hps, KV_LORA, A_NOPE), lambda i, h: (h, 0, 0)),
                  pl.BlockSpec((hps, A_VDIM, KV_LORA), lambda i, h: (h, 0, 0)),
                  pl.BlockSpec((tm, A_ROPE), lambda i, h: (i, 0))],
        out_specs=[pl.BlockSpec((hps, tm, A_NOPE + A_ROPE), lambda i, h: (h, i, 0)),
                   pl.BlockSpec((hps, tm // blk, 2 * A_VDIM, blk), lambda i, h: (h, i, 0, 0))],
        out_shape=[jax.ShapeDtypeStruct((A_HEADS, m, A_NOPE + A_ROPE), BF16),
                   jax.ShapeDtypeStruct((A_HEADS, m // blk, 2 * A_VDIM, blk), BF16)],
        compiler_params=_params(2),
        name="kv_prep",
    )(ckv_b, wk, wvt, krope)


def _softmax_step(s, v, m, l, acc):
    m_new = jnp.maximum(m, jnp.max(s, axis=1, keepdims=True))
    p = jnp.exp(s - m_new)
    alpha = jnp.exp(m - m_new)
    l = alpha * l + jnp.sum(p, axis=1, keepdims=True)
    acc = alpha * acc + _dot(p.astype(BF16), v)
    return m_new, l, acc


def _flash_kernel(q_ref, k_ref, vt_ref, o_ref, m_ref, acc_ref, *, blk, hps):
    qi = pl.program_id(2)
    m_ref[...] = jnp.full_like(m_ref, -jnp.inf)
    acc_ref[...] = jnp.zeros_like(acc_ref)

    def block(kj, diagonal):
        start = pl.multiple_of(kj * blk, blk)
        scores = [_dot_nt(k_ref[h, pl.ds(start, blk), :], q_ref[h]) for h in range(hps)]
        if diagonal:
            key = lax.broadcasted_iota(jnp.int32, (blk, blk), 0)
            qry = lax.broadcasted_iota(jnp.int32, (blk, blk), 1)
            scores = [jnp.where(key <= qry, s, -jnp.inf) for s in scores]
        probs, alphas = [], []
        for h, s in enumerate(scores):
            m_old = m_ref[h]
            m_new = jnp.maximum(m_old, jnp.max(s, axis=0, keepdims=True))
            probs.append(jnp.exp(s - m_new).astype(BF16))
            alphas.append(jnp.exp(m_old - m_new))
            m_ref[h] = m_new
        for h in range(hps):
            acc_ref[h] = alphas[h] * acc_ref[h] + _dot(vt_ref[h, kj], probs[h])

    def body(kj, carry):
        block(kj, False)
        return carry

    lax.fori_loop(0, qi, body, 0)
    block(qi, True)
    for h in range(hps):
        acc = acc_ref[h]
        out_t = acc[:A_VDIM] / acc[A_VDIM:A_VDIM + 1]
        o_ref[:, h * A_VDIM:(h + 1) * A_VDIM] = out_t.T.astype(o_ref.dtype)


def flash_prompt(q, k, vt, batch, seq, blk):
    nq = seq // blk
    dqk = q.shape[2]
    hps = ATT_HEADS_PER_STEP
    assert A_HEADS % hps == 0
    return pl.pallas_call(
        functools.partial(_flash_kernel, blk=blk, hps=hps),
        grid=(batch, A_HEADS // hps, nq),
        in_specs=[pl.BlockSpec((hps, blk, dqk), lambda b, h, i: (h, b * nq + i, 0)),
                  pl.BlockSpec((hps, seq, dqk), lambda b, h, i: (h, b, 0)),
                  pl.BlockSpec((hps, nq, 2 * A_VDIM, blk), lambda b, h, i: (h, b, 0, 0))],
        out_specs=pl.BlockSpec((blk, hps * A_VDIM), lambda b, h, i: (b * nq + i, h)),
        out_shape=jax.ShapeDtypeStruct((batch * seq, A_HEADS * A_VDIM), BF16),
        scratch_shapes=[pltpu.VMEM((hps, 1, blk), F32), pltpu.VMEM((hps, 2 * A_VDIM, blk), F32)],
        compiler_params=_params(3),
        name="flash_prompt",
    )(q, k, vt)


def _paged_kernel(pt_ref, q_ref, ckv_new_ref, kr_new_ref, pool_ckv, pool_kr, o_ref,
                  ckv_buf, kr_buf, sem, m_ref, l_ref, acc_ref, *, n_steps, dec_seq):
    pp = PAGES_PER_STEP
    g = pl.program_id(1)
    step = pl.program_id(0) * n_steps + g
    last_step = pl.num_programs(0) * n_steps - 1
    slot = step % 2

    def page_copies(s, slot_idx):
        copies = []
        for i in range(pp):
            page = pt_ref[s * pp + i]
            copies.append(pltpu.make_async_copy(pool_ckv.at[page], ckv_buf.at[slot_idx, i], sem.at[slot_idx, 0]))
            copies.append(pltpu.make_async_copy(pool_kr.at[page], kr_buf.at[slot_idx, i], sem.at[slot_idx, 1]))
        return copies

    @pl.when(step == 0)
    def _():
        for c in page_copies(0, 0):
            c.start()

    nxt = jnp.minimum(step + 1, last_step)
    for c in page_copies(nxt, 1 - slot):
        c.start()
    for c in page_copies(step, slot):
        c.wait()
    ckv_refs = [ckv_buf.at[slot, i] for i in range(pp)]
    kr_refs = [kr_buf.at[slot, i] for i in range(pp)]

    @pl.when(g == 0)
    def _():
        m_ref[...] = jnp.full_like(m_ref, -jnp.inf)
        l_ref[...] = jnp.zeros_like(l_ref)
        acc_ref[...] = jnp.zeros_like(acc_ref)

    q = q_ref[...]
    q_lat, q_rope = q[:, :KV_LORA], q[:, KV_LORA:]
    grp = PAGES_PER_GROUP
    n_chain = pp // grp
    kcs, scores, probs, alphas = {}, {}, {}, {}

    def score(c):
        lo = c * grp
        kcs[c] = jnp.concatenate([r[...].astype(BF16) for r in ckv_refs[lo:lo + grp]], axis=0)
        krt = jnp.concatenate([r[...].astype(BF16) for r in kr_refs[lo:lo + grp]], axis=1)
        scores[c] = _dot_nt(q_lat, kcs[c]) + _dot(q_rope, krt)

    def stats(c):
        m_old = m_ref[c]
        m_new = jnp.maximum(m_old, jnp.max(scores[c], axis=1, keepdims=True))
        p = jnp.exp(scores[c] - m_new)
        alphas[c] = jnp.exp(m_old - m_new)
        l_ref[c] = alphas[c] * l_ref[c] + jnp.sum(p, axis=1, keepdims=True)
        m_ref[c] = m_new
        probs[c] = p.astype(BF16)

    def value(c):
        acc_ref[c] = alphas[c] * acc_ref[c] + _dot(probs[c], kcs[c])

    for c in range(n_chain + 2):
        if c < n_chain:
            score(c)
        if 0 <= c - 1 < n_chain:
            stats(c - 1)
        if 0 <= c - 2 < n_chain:
            value(c - 2)

    @pl.when(g == n_steps - 1)
    def _():
        def pad_rows(x):
            return jnp.concatenate([x, jnp.zeros((PAGE_SIZE - dec_seq, x.shape[1]), x.dtype)], axis=0)

        kcn = pad_rows(ckv_new_ref[...]).astype(BF16)
        krn = pad_rows(kr_new_ref[...]).astype(BF16)
        sn = _dot_nt(q_lat, kcn) + _dot_nt(q_rope, krn)
        tok = lax.broadcasted_iota(jnp.int32, sn.shape, 0) % dec_seq
        key = lax.broadcasted_iota(jnp.int32, sn.shape, 1)
        sn = jnp.where(key <= tok, sn, -jnp.inf)
        m, l, acc = _softmax_step(sn, kcn, m_ref[0], l_ref[0], acc_ref[0])
        for c in range(1, pp // grp):
            m_c = m_ref[c]
            m_new = jnp.maximum(m, m_c)
            a, b = jnp.exp(m - m_new), jnp.exp(m_c - m_new)
            l = a * l + b * l_ref[c]
            acc = a * acc + b * acc_ref[c]
            m = m_new
        o_ref[...] = (acc / l).astype(o_ref.dtype)

    @pl.when(step == last_step)
    def _():
        for c in page_copies(last_step, 1 - slot):
            c.wait()


def paged_attention(q, pool_ckv, pool_kr, page_table, ckv_new, kr_new):
    nb, rows, dq = q.shape
    n_pages = page_table.shape[1]
    dec_seq = ckv_new.shape[1]
    pp = PAGES_PER_STEP
    assert n_pages % pp == 0 and pp % PAGES_PER_GROUP == 0
    n_steps = n_pages // pp
    page = pool_ckv.shape[1]
    assert page == PAGE_SIZE and pool_kr.shape[1:] == (A_ROPE, page)

    in_specs = [pl.BlockSpec((None, rows, dq), lambda b, g, pt: (b, 0, 0)),
                pl.BlockSpec((None, dec_seq, KV_LORA), lambda b, g, pt: (b, 0, 0)),
                pl.BlockSpec((None, dec_seq, A_ROPE), lambda b, g, pt: (b, 0, 0)),
                pl.BlockSpec(memory_space=pl.ANY),
                pl.BlockSpec(memory_space=pl.ANY)]
    n_chain = pp // PAGES_PER_GROUP
    grid_spec = pltpu.PrefetchScalarGridSpec(
        num_scalar_prefetch=1,
        grid=(nb, n_steps),
        in_specs=in_specs,
        out_specs=pl.BlockSpec((None, rows, KV_LORA), lambda b, g, pt: (b, 0, 0)),
        scratch_shapes=[pltpu.VMEM((2, pp, page, KV_LORA), F32),
                        pltpu.VMEM((2, pp, A_ROPE, page), F32),
                        pltpu.SemaphoreType.DMA((2, 2)),
                        pltpu.VMEM((n_chain, rows, 1), F32),
                        pltpu.VMEM((n_chain, rows, 1), F32),
                        pltpu.VMEM((n_chain, rows, KV_LORA), F32)],
    )
    return pl.pallas_call(
        functools.partial(_paged_kernel, n_steps=n_steps, dec_seq=dec_seq),
        grid_spec=grid_spec,
        out_shape=jax.ShapeDtypeStruct((nb, rows, KV_LORA), BF16),
        compiler_params=_params(2),
        name="paged_attention",
    )(page_table.reshape(-1), q, ckv_new, kr_new, pool_ckv, pool_kr)


def _head_mm_kernel(x_ref, w_ref, o_ref):
    o_ref[...] = _dot(x_ref[...], w_ref[...]).astype(o_ref.dtype)


def head_matmul(x, w):
    nh, m, kd = x.shape
    n = w.shape[2]
    return pl.pallas_call(
        _head_mm_kernel,
        grid=(nh,),
        in_specs=[pl.BlockSpec((None, m, kd), lambda h: (h, 0, 0)),
                  pl.BlockSpec((None, kd, n), lambda h: (h, 0, 0))],
        out_specs=pl.BlockSpec((m, n), lambda h: (0, h)),
        out_shape=jax.ShapeDtypeStruct((m, nh * n), BF16),
        compiler_params=_params(1),
        name="head_matmul",
    )(x, w)


def _mlstm_kernel(q_ref, k_ref, v_ref, if_ref, mo_ref, bias_ref, gmh_ref, c0_ref, n0_ref, m0_ref,
                  o_ref, c_ref, n_ref, m_ref, *, rows, lp, nseq):
    chunk = pl.program_id(1)

    @pl.when(chunk == 0)
    def _():
        c_ref[...] = c0_ref[...]
        n_ref[...] = n0_ref[...]
        m_ref[...] = m0_ref[...]

    def pad(x):
        if rows == lp:
            return x
        return jnp.concatenate([x, jnp.zeros((lp - rows, x.shape[1]), x.dtype)], axis=0)

    t_idx = lax.broadcasted_iota(jnp.int32, (lp, lp), 0)
    s_idx = lax.broadcasted_iota(jnp.int32, (lp, lp), 1)
    causal = s_idx <= t_idx
    diag = s_idx == t_idx
    valid = lax.broadcasted_iota(jnp.int32, (lp, 1), 0) < rows
    for sq, h in [(sq, h) for sq in range(nseq) for h in range(M_HEADS)]:
        r0 = sq * rows
        gates = pad(if_ref[r0:r0 + rows, :] + bias_ref[...])
        q = pad(q_ref[r0:r0 + rows, h * M_DQK:(h + 1) * M_DQK]) * (M_DQK ** -0.5)
        k = pad(k_ref[r0:r0 + rows, h * M_DQK:(h + 1) * M_DQK])
        v = pad(v_ref[r0:r0 + rows, h * M_DV:(h + 1) * M_DV])
        i_col = gates[:, h:h + 1]
        f_pre = gates[:, M_HEADS + h:M_HEADS + h + 1]
        f_col = jnp.minimum(f_pre, 0.0) - jnp.log1p(jnp.exp(-jnp.abs(f_pre)))
        if rows != lp:
            i_col = jnp.where(valid, i_col, -jnp.inf)
            f_col = jnp.where(valid, f_col, 0.0)

        f_row = jnp.sum(jnp.where(diag, f_col, 0.0), axis=0, keepdims=True)
        i_row = jnp.sum(jnp.where(diag, i_col, 0.0), axis=0, keepdims=True)
        b_col = jnp.sum(jnp.where(causal, f_row, 0.0), axis=1, keepdims=True)
        b_row = jnp.sum(jnp.where(t_idx <= s_idx, f_col, 0.0), axis=0, keepdims=True)

        m_prev = m_ref[sq, h]
        c_prev = c_ref[sq, h]
        n_prev = n_ref[sq, h]
        a_col = b_col + m_prev
        dmat = jnp.where(causal, b_col - b_row + i_row, -jnp.inf)
        mt = jnp.maximum(a_col, jnp.max(dmat, axis=1, keepdims=True))
        w_inter = jnp.exp(a_col - mt)
        qb, kb, vb = q.astype(BF16), k.astype(BF16), v.astype(BF16)
        qk = _dot_nt(qb, kb) * jnp.exp(dmat - mt)
        num = _dot(qk.astype(BF16), vb) + w_inter * _dot(qb, c_prev.astype(BF16))
        den = jnp.sum(qk, axis=1, keepdims=True) + w_inter * jnp.sum(q * n_prev, axis=1, keepdims=True)
        hid = num / jnp.maximum(jnp.abs(den), jnp.exp(-mt))

        b_last = b_col[lp - 1:lp, :]
        g_col = b_last - b_col + i_col
        m_new = jnp.maximum(b_last + m_prev, jnp.max(g_col, axis=0, keepdims=True))
        decay = jnp.exp(b_last + m_prev - m_new)
        kw = jnp.exp(g_col - m_new) * k
        c_ref[sq, h] = decay * c_prev + _dot_tn(kw.astype(BF16), vb)
        n_ref[sq, h] = decay * n_prev + jnp.sum(kw, axis=0, keepdims=True)
        m_ref[sq, h] = m_new

        hn = hid * lax.rsqrt(jnp.mean(hid * hid, axis=1, keepdims=True) + EPS) * gmh_ref[h]
        out = jax.nn.sigmoid(pad(mo_ref[r0:r0 + rows, h * M_DV:(h + 1) * M_DV])) * hn
        o_ref[r0:r0 + rows, h * M_DV:(h + 1) * M_DV] = out[:rows].astype(o_ref.dtype)


def mlstm(proj, lay, bias_row, g_mh, c0, n0, m0, batch, seq):
    rows = _tile(seq, MLSTM_CHUNK, 8)
    lp = max(rows, MLSTM_MIN_ROWS)
    nc = seq // rows
    nseq = MLSTM_SEQS_PER_STEP if (nc == 1 and batch % MLSTM_SEQS_PER_STEP == 0) else 1
    tr = nseq * rows
    hq, hv = M_HEADS * M_DQK, M_HEADS * M_DV
    oq, ok, ov, oi, oo = (lay[n][0] for n in ("mq", "mk", "mv", "mif", "mo"))
    assert oq % hq == 0 and ok % hq == 0 and ov % hv == 0 and oo % hv == 0 and oi % LANES == 0

    def rowmap(col_block):
        return lambda b, c: (b * nc + c, col_block)

    def state_spec(*dims):
        return pl.BlockSpec((nseq, M_HEADS) + dims, lambda b, c: (b, 0, 0, 0))

    out, c1, n1, m1 = pl.pallas_call(
        functools.partial(_mlstm_kernel, rows=rows, lp=lp, nseq=nseq),
        grid=(batch // nseq, nc),
        in_specs=[pl.BlockSpec((tr, hq), rowmap(oq // hq)),
                  pl.BlockSpec((tr, hq), rowmap(ok // hq)),
                  pl.BlockSpec((tr, hv), rowmap(ov // hv)),
                  pl.BlockSpec((tr, LANES), rowmap(oi // LANES)),
                  pl.BlockSpec((tr, hv), rowmap(oo // hv)),
                  pl.BlockSpec((1, LANES), lambda b, c: (0, 0)),
                  pl.BlockSpec((M_HEADS, 1, M_DV), lambda b, c: (0, 0, 0)),
                  state_spec(M_DQK, M_DV), state_spec(1, M_DQK), state_spec(1, 1)],
        out_specs=[pl.BlockSpec((tr, hv), lambda b, c: (b * nc + c, 0)),
                   state_spec(M_DQK, M_DV), state_spec(1, M_DQK), state_spec(1, 1)],
        out_shape=[jax.ShapeDtypeStruct((batch * seq, hv), BF16),
                   jax.ShapeDtypeStruct((batch, M_HEADS, M_DQK, M_DV), F32),
                   jax.ShapeDtypeStruct((batch, M_HEADS, 1, M_DQK), F32),
                   jax.ShapeDtypeStruct((batch, M_HEADS, 1, 1), F32)],
        compiler_params=_params(2),
        name="mlstm",
    )(proj, proj, proj, proj, proj, bias_row, g_mh.reshape(M_HEADS, 1, M_DV),
      c0, n0.reshape(batch, M_HEADS, 1, M_DQK), m0.reshape(batch, M_HEADS, 1, 1))
    return out, c1, n1.reshape(batch, M_HEADS, M_DQK), m1.reshape(batch, M_HEADS)


def _ffn_up_kernel(x_ref, wg_ref, wv_ref, c0_ref, w_ref, b_ref, act_ref, tail_ref, wg_bf, wv_bf, *carry,
                   nb, tiles_per_seq):
    i = pl.program_id(1)
    _cast_weights_once([wg_ref, wv_ref], [wg_bf, wv_bf])
    tm, tc = act_ref.shape
    nsub = EPILOGUE_SUBTILES if (nb == 1 or nb % EPILOGUE_SUBTILES == 0) else 1
    rm = tm // nsub
    nbs, st = (1, rm) if nb == 1 else (nb // nsub, tm // nb)
    gates = [_dot(x_ref[s * rm:(s + 1) * rm, :], wg_bf[...]).reshape(nbs, st, tc)
             for s in range(nsub)]
    tails = [g[:, st - (CONV_W - 1):, :] for g in gates]
    last_rows = tails[-1] if nb == 1 else jnp.concatenate(tails, axis=0)
    prev = c0_ref[...]
    if tiles_per_seq > 1:
        carry_ref = carry[0]

        @pl.when(i == 0)
        def _():
            carry_ref[...] = jnp.zeros_like(carry_ref)

        prev = jnp.where(i % tiles_per_seq == 0, prev, carry_ref[...])
        carry_ref[...] = last_rows
    t = lax.broadcasted_iota(jnp.int32, (nbs, st, tc), 1)
    w = w_ref[...]
    for s in range(nsub):
        g = gates[s]
        if nb == 1:
            before = prev if s == 0 else tails[s - 1]
        else:
            before = prev[s * nbs:(s + 1) * nbs]
        prev1 = jnp.where(t == 0, before[:, 1:2, :], pltpu.roll(g, 1, 1))
        prev2 = jnp.where(t == 0, before[:, 0:1, :], jnp.where(t == 1, before[:, 1:2, :], pltpu.roll(g, 2, 1)))
        conv = b_ref[...] + w[0:1, :] * prev2 + w[1:2, :] * prev1 + w[2:3, :] * g
        val = _dot(x_ref[s * rm:(s + 1) * rm, :], wv_bf[...]).reshape(nbs, st, tc)
        act = jax.nn.gelu(conv, approximate=True) * val
        act_ref[s * rm:(s + 1) * rm, :] = act.reshape(rm, tc).astype(act_ref.dtype)
    tail_ref[...] = last_rows


def ffn_up_convglu(x, w_up, conv0, w_conv, b_conv, batch, seq):
    assert CONV_W == 3 and seq >= CONV_W - 1
    m, kd = x.shape
    tc = _tile(D_FF, 256)
    ncol = D_FF // tc
    if seq >= ROW_TILE:
        tm, nb = _tile(seq, ROW_TILE, 8), 1
    else:
        nb = max(1, min(batch, ROW_TILE // seq))
        assert batch % nb == 0
        tm = nb * seq
    tps = seq // (tm // nb)
    assert w_up.dtype == F32
    scratch = [pltpu.VMEM((kd, tc), BF16), pltpu.VMEM((kd, tc), BF16)]
    if tps > 1:
        scratch.append(pltpu.VMEM((nb, CONV_W - 1, tc), F32))
    act, tail = pl.pallas_call(
        functools.partial(_ffn_up_kernel, nb=nb, tiles_per_seq=tps),
        grid=(ncol, m // tm),
        in_specs=[pl.BlockSpec((tm, kd), lambda j, i: (i, 0)),
                  pl.BlockSpec((kd, tc), lambda j, i: (0, j)),
                  pl.BlockSpec((kd, tc), lambda j, i: (0, j + ncol)),
                  pl.BlockSpec((nb, CONV_W - 1, tc), lambda j, i: (i // tps, 0, j)),
                  pl.BlockSpec((CONV_W, tc), lambda j, i: (0, j)),
                  pl.BlockSpec((1, tc), lambda j, i: (0, j))],
        out_specs=[pl.BlockSpec((tm, tc), lambda j, i: (i, j)),
                   pl.BlockSpec((nb, CONV_W - 1, tc), lambda j, i: (i, 0, j))],
        out_shape=[jax.ShapeDtypeStruct((m, D_FF), BF16),
                   jax.ShapeDtypeStruct((batch * tps, CONV_W - 1, D_FF), F32)],
        scratch_shapes=scratch,
        compiler_params=_params(2),
        name="ffn_up_convglu",
    )(x, w_up, w_up, conv0, w_conv, b_conv.reshape(1, D_FF))
    return act, tail.reshape(batch, tps, CONV_W - 1, D_FF)[:, tps - 1]


_MODE_COPY, _MODE_DUP_ROPE, _MODE_GATES, _MODE_ZERO = 0, 1, 2, 3


def _relayout_kernel(off_ref, mode_ref, src_ref, o_ref):
    mode = mode_ref[pl.program_id(0)]
    row = lax.broadcasted_iota(jnp.int32, src_ref.shape, 0)

    @pl.when(mode == _MODE_COPY)
    def _():
        o_ref[...] = src_ref[...].T.astype(o_ref.dtype)

    @pl.when(mode == _MODE_DUP_ROPE)
    def _():
        kr = src_ref[:A_ROPE, :]
        o_ref[...] = jnp.concatenate([kr, kr], axis=0).T.astype(o_ref.dtype)

    @pl.when(mode == _MODE_GATES)
    def _():
        o_ref[...] = jnp.where(row < 2 * M_HEADS, src_ref[...], 0.0).T.astype(o_ref.dtype)

    @pl.when(mode == _MODE_ZERO)
    def _():
        o_ref[...] = jnp.zeros_like(o_ref)


def relayout_w_in(w_in_t, lay, total, src_off):
    assert 2 * A_ROPE == LANES and 2 * M_HEADS <= LANES
    n_tiles = total // LANES
    off, mode = np.zeros(n_tiles, np.int32), np.full(n_tiles, _MODE_ZERO, np.int32)
    groups = [(lay[n][0], lay[n][1], src_off[n], _MODE_COPY) for n in ("cq", "mq", "mk", "ckv", "mv", "mo", "ga", "gb")]
    groups += [(*lay["kr2"], src_off["kr"], _MODE_DUP_ROPE), (*lay["mif"], src_off["mi"], _MODE_GATES)]
    n_src, rows = w_in_t.shape
    for dst, width, s0, md in groups:
        assert dst % LANES == 0 and width % LANES == 0 and s0 % 8 == 0
        for t in range(width // LANES):
            off[dst // LANES + t], mode[dst // LANES + t] = s0 + t * LANES, md
    assert int(off.max()) + LANES <= n_src
    return pl.pallas_call(
        _relayout_kernel,
        grid_spec=pltpu.PrefetchScalarGridSpec(
            num_scalar_prefetch=2,
            grid=(n_tiles,),
            in_specs=[pl.BlockSpec((pl.Element(LANES), pl.Element(rows)),
                                   lambda t, off8, mode: (pl.multiple_of(off8[t] * 8, 8), 0))],
            out_specs=pl.BlockSpec((rows, LANES), lambda t, off, mode: (0, t)),
        ),
        out_shape=jax.ShapeDtypeStruct((rows, total), BF16),
        compiler_params=_params(1),
        name="relayout_w_in",
    )(jnp.asarray(off // 8), jnp.asarray(mode), w_in_t)


def _prepare_weights(w):
    lay, total = _padded_layout()
    offs = np.concatenate([[0], np.cumsum([Q_LORA, KV_LORA, A_ROPE, M_HEADS * M_DQK, M_HEADS * M_DQK,
                                            M_HEADS * M_DV, M_HEADS, M_HEADS, M_HEADS * M_DV, D_MODEL, D_MODEL])])
    src = {n: (int(offs[i]), int(offs[i + 1])) for i, n in enumerate(
        ["cq", "ckv", "kr", "mq", "mk", "mv", "mi", "mf", "mo", "ga", "gb"])}
    assert src["mf"][0] == src["mi"][1]
    w_in_p = relayout_w_in(jnp.swapaxes(w["w_in"], 0, 1), lay, total, {n: s[0] for n, s in src.items()})

    w_uq = w["w_uq"]
    rope_cols = w_uq[..., A_NOPE:]
    wq = jnp.concatenate([w_uq[..., :A_NOPE], rope_cols, rope_cols], axis=-1).transpose(1, 0, 2).astype(BF16)
    bias_row = jnp.concatenate([w["b_i"].astype(F32), w["b_f"].astype(F32),
                                jnp.zeros((LANES - 2 * M_HEADS,), F32)]).reshape(1, LANES)
    return {
        "lay": lay, "n_proj": total, "w_in_p": w_in_p, "wq": wq,
        "wuk_t": w["w_uk"].transpose(1, 2, 0).astype(BF16),
        "wk": w["w_uk"].transpose(1, 0, 2).astype(BF16),
        "wv": w["w_uv"].transpose(1, 0, 2).astype(BF16),
        "wvt": w["w_uv"].transpose(1, 2, 0).astype(BF16),
        "bias_row": bias_row,
        "w_br_a": w["w_br_a"].astype(F32), "w_br_b": w["w_br_b"].astype(F32),
        "w_o": w["w_o"].astype(F32), "w_up": w["w_up"].astype(F32), "w_down": w["w_down"].astype(BF16),
        "w_ple_gate": w["w_ple_gate"].astype(F32), "w_ple_proj": w["w_ple_proj"].astype(F32),
    }


def _rope_tables(pos, batch):
    half = A_ROPE // 2
    inv = jnp.power(ROPE_THETA, -jnp.arange(half, dtype=F32) / half)
    ang = pos[:, None] * inv[None, :]
    cos, sin = jnp.cos(ang), jnp.sin(ang)
    reps = LANES // A_ROPE
    cos_t = jnp.tile(jnp.concatenate([cos, cos], axis=1), (batch, reps))
    sin_t = jnp.tile(jnp.concatenate([-sin, sin], axis=1), (batch, reps))
    return cos_t, sin_t


def _hybrid_layer(x, pe, pos, w, pw, c0, n0, m0, conv0, paged):
    batch, seq, _ = x.shape
    m = batch * seq
    lay = pw["lay"]
    x2 = x.reshape(m, D_MODEL)
    cos, sin = _rope_tables(pos, batch)

    h = rmsnorm_rows(x2, w["g_mix"], BF16)
    proj = fused_matmul([(h, pw["w_in_p"])], [], _ep_plain, pw["n_proj"], F32,
                        tn=_tile(pw["n_proj"], 1024), name="in_proj")
    cqn, c_kv, ckv_b, k_rope = mla_prep(proj, lay, w["g_qa"], w["g_kva"], cos, sin)

    if paged is None:
        q = q_prep(cqn, pw["wq"], cos, sin)
        blk = _tile(seq, ATT_BLOCK)
        k, vt = kv_prep(ckv_b, k_rope, pw["wk"], pw["wvt"], blk)
        a_out = flash_prompt(q, k, vt, batch, seq, blk)
    else:
        pool_ckv, pool_kr, page_table = paged
        q = q_prep(cqn, pw["wq"], cos, sin, pw["wuk_t"])
        dq = q.shape[2]
        q = q.reshape(A_HEADS, batch, seq, dq).transpose(1, 0, 2, 3).reshape(batch, A_HEADS * seq, dq)
        o_lat = paged_attention(q, pool_ckv, pool_kr, page_table,
                                c_kv.reshape(batch, seq, KV_LORA), k_rope.reshape(batch, seq, A_ROPE))
        o_lat = o_lat.reshape(batch, A_HEADS, seq, KV_LORA).transpose(1, 0, 2, 3).reshape(A_HEADS, m, KV_LORA)
        a_out = head_matmul(o_lat, pw["wv"])

    b_out, c1, n1, m1 = mlstm(proj, lay, pw["bias_row"], w["g_mh"], c0, n0, m0, batch, seq)

    merged = fused_matmul([(a_out, pw["w_br_a"]), (b_out, pw["w_br_b"])],
                          [(proj, lay["ga"][0]), (proj, lay["gb"][0])],
                          _ep_gated_merge, D_MODEL, BF16, name="branch_merge")
    x2 = fused_matmul([(merged, pw["w_o"])], [(x2, 0)], _ep_residual, D_MODEL, F32, name="out_proj")

    hf = rmsnorm_rows(x2, w["g_ffn"], BF16)
    act, conv_tail = ffn_up_convglu(hf, pw["w_up"], conv0.astype(F32), w["w_conv"], w["b_conv"], batch, seq)
    x2 = fused_matmul([(act, pw["w_down"])], [(x2, 0)], _ep_residual, D_MODEL, F32,
                      tn=_tile(D_MODEL, 512), tk=_tile(D_FF, D_FF // 2), name="ffn_down")

    hp = rmsnorm_rows(x2, w["g_ple"], BF16)
    x2 = fused_matmul([(hp, pw["w_ple_gate"]), (pe.reshape(m, PLE_DIM).astype(BF16), pw["w_ple_proj"])],
                      [(x2, 0)], _ep_ple, D_MODEL, F32, name="ple")
    return (x2, c_kv.reshape(batch, seq, KV_LORA), k_rope.reshape(batch, seq, A_ROPE), c1, n1, m1, conv_tail)


def kernel(x_prompt, x_sample, cache_ckv, cache_krope, state_C, state_n, state_m, state_conv, page_table,
           p_prompt, p_sample, g_mix, w_in, b_i, b_f, g_qa, w_uq, g_kva, w_uk, w_uv, g_mh, w_br_a, w_br_b,
           w_o, g_ffn, w_up, w_conv, b_conv, w_down, g_ple, w_ple_gate, w_ple_proj, g_final):
    bp, sp = x_prompt.shape[0], x_prompt.shape[1]
    bs, ss = x_sample.shape[0], x_sample.shape[1]
    depth = w_in.shape[0]
    past = page_table.shape[1] * cache_ckv.shape[2]
    pos_p = jnp.arange(sp, dtype=F32)
    pos_s = past + jnp.arange(ss, dtype=F32)
    c0p = jnp.zeros((bp, M_HEADS, M_DQK, M_DV), F32)
    n0p = jnp.zeros((bp, M_HEADS, M_DQK), F32)
    m0p = jnp.zeros((bp, M_HEADS), F32)
    conv0p = jnp.zeros((bp, CONV_W - 1, D_FF), F32)
    names = ["g_mix", "w_in", "b_i", "b_f", "g_qa", "w_uq", "g_kva", "w_uk", "w_uv", "g_mh", "w_br_a", "w_br_b",
             "w_o", "g_ffn", "w_up", "w_conv", "b_conv", "w_down", "g_ple", "w_ple_gate", "w_ple_proj"]
    stacked = [g_mix, w_in, b_i, b_f, g_qa, w_uq, g_kva, w_uk, w_uv, g_mh, w_br_a, w_br_b,
               w_o, g_ffn, w_up, w_conv, b_conv, w_down, g_ple, w_ple_gate, w_ple_proj]
    xp, xs = x_prompt, x_sample
    outs_p = [[] for _ in range(6)]
    outs_s = [[] for _ in range(6)]
    for l in range(depth):
        w = {n: a[l] for n, a in zip(names, stacked)}
        pw = _prepare_weights(w)
        xp2, *new_p = _hybrid_layer(xp, p_prompt[l], pos_p, w, pw, c0p, n0p, m0p, conv0p, None)
        xs2, *new_s = _hybrid_layer(xs, p_sample[l], pos_s, w, pw, state_C[l], state_n[l], state_m[l],
                                    state_conv[l],
                                    (cache_ckv[l], jnp.swapaxes(cache_krope[l], 1, 2), page_table))
        xp, xs = xp2.reshape(xp.shape), xs2.reshape(xs.shape)
        for j in range(6):
            outs_p[j].append(new_p[j])
            outs_s[j].append(new_s[j])
    stack_p = [jnp.stack(o) for o in outs_p]
    stack_s = [jnp.stack(o) for o in outs_s]
    y_prompt = rmsnorm_rows(xp.reshape(bp * sp, D_MODEL), g_final, F32).reshape(xp.shape)
    y_sample = rmsnorm_rows(xs.reshape(bs * ss, D_MODEL), g_final, F32).reshape(xs.shape)
    return (y_prompt, y_sample, *stack_p, *stack_s)
```

```python
import functools

import jax
import jax.numpy as jnp
import numpy as np
from jax import lax
from jax.experimental import pallas as pl
from jax.experimental.pallas import tpu as pltpu

D_MODEL = 4096
DEPTH = 1
PAGE_SIZE = 128
A_HEADS = 16
A_NOPE = 128
A_ROPE = 64
A_VDIM = 128
Q_LORA = 1024
KV_LORA = 512
ROPE_THETA = 10000.0
A_SCALE = (A_NOPE + A_ROPE) ** -0.5
M_HEADS = 4
M_DQK = 256
M_DV = 512
D_FF = 11008
CONV_W = 3
PLE_DIM = 256
EPS = 1e-6

LANES = 128
VMEM_LIMIT = 56 * 1024 * 1024
ROW_TILE = 1024
COL_TILE = 512
EPILOGUE_SUBTILES = 4
MLSTM_CHUNK = 256
MLSTM_MIN_ROWS = 128
MLSTM_SEQS_PER_STEP = 4
ATT_BLOCK = 256
ATT_HEADS_PER_STEP = 8
PREP_HEADS_PER_STEP = 4
PAGES_PER_STEP = 64
PAGES_PER_GROUP = 16

F32 = jnp.float32
BF16 = jnp.bfloat16


def _params(n_grid):
    return pltpu.CompilerParams(dimension_semantics=("arbitrary",) * n_grid,
                                vmem_limit_bytes=VMEM_LIMIT)


def _tile(n, pref, unit=LANES):
    if n <= pref:
        return n
    best = None
    for t in range(unit, pref + 1, unit):
        if n % t == 0:
            best = t
    assert best is not None, (n, pref)
    return best


def _padded_layout():
    hq, hv = M_HEADS * M_DQK, M_HEADS * M_DV
    widths = [("cq", Q_LORA), ("mq", hq), ("mk", hq), ("ckv", KV_LORA), ("kr2", 2 * A_ROPE), ("mif", LANES)]
    off, lay = 0, {}
    for name, w in widths:
        lay[name] = (off, w)
        off += w
    off = -(-off // hv) * hv
    for name, w in [("mv", hv), ("mo", hv), ("ga", D_MODEL), ("gb", D_MODEL)]:
        lay[name] = (off, w)
        off += w
    return lay, off


def _dot(a, b):
    return jnp.dot(a, b, preferred_element_type=F32)


def _dot_nt(a, b):
    return lax.dot_general(a, b, (((1,), (1,)), ((), ())), preferred_element_type=F32)


def _dot_tn(a, b):
    return lax.dot_general(a, b, (((0,), (0,)), ((), ())), preferred_element_type=F32)


def _rmsnorm_kernel(x_ref, g_ref, o_ref):
    x = x_ref[...]
    y = x * lax.rsqrt(jnp.mean(x * x, axis=-1, keepdims=True) + EPS)
    o_ref[...] = (y * g_ref[...]).astype(o_ref.dtype)


def rmsnorm_rows(x, g, out_dtype):
    m, d = x.shape
    tm = _tile(m, 512, 8)
    return pl.pallas_call(
        _rmsnorm_kernel,
        grid=(m // tm,),
        in_specs=[pl.BlockSpec((tm, d), lambda i: (i, 0)),
                  pl.BlockSpec((1, d), lambda i: (0, 0))],
        out_specs=pl.BlockSpec((tm, d), lambda i: (i, 0)),
        out_shape=jax.ShapeDtypeStruct((m, d), out_dtype),
        compiler_params=_params(1),
        name="rmsnorm",
    )(x, g.reshape(1, d).astype(F32))


def _cast_weights_once(w_refs, cast_refs):
    @pl.when(pl.program_id(1) == 0)
    def _():
        for w, c in zip(w_refs, cast_refs):
            c[...] = w[...].astype(c.dtype)


def _matmul_kernel(*refs, n_pairs, n_extra, epilogue, nk, n_cast):
    pairs = [(refs[2 * p], refs[2 * p + 1]) for p in range(n_pairs)]
    extras = refs[2 * n_pairs:2 * n_pairs + n_extra]
    o_ref = refs[2 * n_pairs + n_extra]
    if nk == 1:
        if n_cast:
            cast_refs = refs[2 * n_pairs + n_extra + 1:]
            _cast_weights_once([w for _, w in pairs], cast_refs)
            pairs = [(x, c) for (x, _), c in zip(pairs, cast_refs)]
        tm = o_ref.shape[0]
        nsub = EPILOGUE_SUBTILES if (n_extra and tm % (8 * EPILOGUE_SUBTILES) == 0) else 1
        rs = tm // nsub
        accs = [[_dot(x[s * rs:(s + 1) * rs, :], w[...]) for x, w in pairs] for s in range(nsub)]
        for s in range(nsub):
            rows = slice(s * rs, (s + 1) * rs)
            o_ref[rows, :] = epilogue(accs[s], [e[rows, :] for e in extras]).astype(o_ref.dtype)
        return
    acc_ref = refs[2 * n_pairs + n_extra + 1]
    k = pl.program_id(2)
    x, w = pairs[0]

    @pl.when(k == 0)
    def _():
        acc_ref[...] = jnp.zeros_like(acc_ref)

    acc_ref[...] += _dot(x[...], w[...])

    @pl.when(k == nk - 1)
    def _():
        o_ref[...] = epilogue([acc_ref[...]], [e[...] for e in extras]).astype(o_ref.dtype)


def fused_matmul(pairs, extras, epilogue, n_out, out_dtype, *, tm=None, tn=None, tk=None, name="matmul"):
    m = pairs[0][0].shape[0]
    tm = tm or _tile(m, ROW_TILE, 8)
    tn = tn or _tile(n_out, COL_TILE)
    kdim = pairs[0][0].shape[1]
    nk = 1 if tk is None else kdim // tk
    assert nk == 1 or len(pairs) == 1
    cast = any(w.dtype != BF16 for _, w in pairs)
    assert not cast or (nk == 1 and all(w.dtype == F32 for _, w in pairs))

    def order(f):
        return (lambda j, i: f(i, j)) if cast else f

    in_specs, args = [], []
    for x, w in pairs:
        kd = x.shape[1]
        if nk == 1:
            in_specs += [pl.BlockSpec((tm, kd), order(lambda i, j: (i, 0))),
                         pl.BlockSpec((kd, tn), order(lambda i, j: (0, j)))]
        else:
            in_specs += [pl.BlockSpec((tm, tk), lambda i, j, k: (i, k)),
                         pl.BlockSpec((tk, tn), lambda i, j, k: (k, j))]
        args += [x, w]
    for arr, off in extras:
        assert off % tn == 0
        ob = off // tn
        if nk == 1:
            in_specs.append(pl.BlockSpec((tm, tn), order(lambda i, j, ob=ob: (i, j + ob))))
        else:
            in_specs.append(pl.BlockSpec((tm, tn), lambda i, j, k, ob=ob: (i, j + ob)))
        args.append(arr)
    if nk == 1:
        grid = (n_out // tn, m // tm) if cast else (m // tm, n_out // tn)
        out_spec = pl.BlockSpec((tm, tn), order(lambda i, j: (i, j)))
        scratch = [pltpu.VMEM((x.shape[1], tn), BF16) for x, _ in pairs] if cast else []
    else:
        grid = (m // tm, n_out // tn, nk)
        out_spec = pl.BlockSpec((tm, tn), lambda i, j, k: (i, j))
        scratch = [pltpu.VMEM((tm, tn), F32)]
    return pl.pallas_call(
        functools.partial(_matmul_kernel, n_pairs=len(pairs), n_extra=len(extras), epilogue=epilogue, nk=nk,
                          n_cast=len(pairs) if cast else 0),
        grid=grid,
        in_specs=in_specs,
        out_specs=out_spec,
        out_shape=jax.ShapeDtypeStruct((m, n_out), out_dtype),
        scratch_shapes=scratch,
        compiler_params=_params(len(grid)),
        name=name,
    )(*args)


def _ep_plain(accs, extras):
    return accs[0]


def _ep_residual(accs, extras):
    return extras[0] + accs[0]


def _ep_gated_merge(accs, extras):
    return jax.nn.sigmoid(extras[0]) * accs[0] + jax.nn.sigmoid(extras[1]) * accs[1]


def _ep_ple(accs, extras):
    return extras[0] + jax.nn.sigmoid(accs[0]) * accs[1]


def _rope_mix(x, cos, sin):
    return x * cos + pltpu.roll(x, A_ROPE // 2, 1) * sin


def _mla_prep_kernel(cq_ref, ckv_ref, kr_ref, gq_ref, gkv_ref, cos_ref, sin_ref,
                     cqn_ref, ckv_out_ref, ckvb_ref, krope_ref):
    cq = cq_ref[...]
    cqn = cq * lax.rsqrt(jnp.mean(cq * cq, axis=-1, keepdims=True) + EPS) * gq_ref[...]
    cqn_ref[...] = cqn.astype(cqn_ref.dtype)
    ckv = ckv_ref[...]
    ckvn = ckv * lax.rsqrt(jnp.mean(ckv * ckv, axis=-1, keepdims=True) + EPS) * gkv_ref[...]
    ckv_out_ref[...] = ckvn
    ckvb_ref[...] = ckvn.astype(ckvb_ref.dtype)
    krope_ref[...] = _rope_mix(kr_ref[...], cos_ref[...], sin_ref[...])[:, :A_ROPE]


def mla_prep(proj, lay, g_qa, g_kva, cos, sin):
    m = proj.shape[0]
    tm = _tile(m, 512, 8)
    (o_cq, w_cq), (o_ckv, w_ckv), (o_kr, w_kr) = lay["cq"], lay["ckv"], lay["kr2"]
    assert o_cq % w_cq == 0 and o_ckv % w_ckv == 0 and o_kr % w_kr == 0
    return pl.pallas_call(
        _mla_prep_kernel,
        grid=(m // tm,),
        in_specs=[pl.BlockSpec((tm, w_cq), lambda i: (i, o_cq // w_cq)),
                  pl.BlockSpec((tm, w_ckv), lambda i: (i, o_ckv // w_ckv)),
                  pl.BlockSpec((tm, w_kr), lambda i: (i, o_kr // w_kr)),
                  pl.BlockSpec((1, w_cq), lambda i: (0, 0)),
                  pl.BlockSpec((1, w_ckv), lambda i: (0, 0)),
                  pl.BlockSpec((tm, LANES), lambda i: (i, 0)),
                  pl.BlockSpec((tm, LANES), lambda i: (i, 0))],
        out_specs=[pl.BlockSpec((tm, w_cq), lambda i: (i, 0)),
                   pl.BlockSpec((tm, w_ckv), lambda i: (i, 0)),
                   pl.BlockSpec((tm, w_ckv), lambda i: (i, 0)),
                   pl.BlockSpec((tm, A_ROPE), lambda i: (i, 0))],
        out_shape=[jax.ShapeDtypeStruct((m, w_cq), BF16),
                   jax.ShapeDtypeStruct((m, w_ckv), F32),
                   jax.ShapeDtypeStruct((m, w_ckv), BF16),
                   jax.ShapeDtypeStruct((m, A_ROPE), F32)],
        compiler_params=_params(1),
        name="mla_prep",
    )(proj, proj, proj, g_qa.reshape(1, -1), g_kva.reshape(1, -1), cos, sin)


def _q_prep_kernel(x_ref, w_ref, cos_ref, sin_ref, *rest, absorbed, hps):
    x, cos, sin = x_ref[...], cos_ref[...], sin_ref[...]
    for h in range(hps):
        acc = _dot(x, w_ref[h])
        nope = (acc[:, :A_NOPE] * A_SCALE).astype(BF16)
        rot = _rope_mix(acc[:, A_NOPE:], cos, sin)[:, :A_ROPE] * A_SCALE
        if absorbed:
            wuk_ref, o_ref = rest
            o_ref[h, :, :KV_LORA] = _dot(nope, wuk_ref[h]).astype(o_ref.dtype)
            o_ref[h, :, KV_LORA:] = rot.astype(o_ref.dtype)
        else:
            (o_ref,) = rest
            o_ref[h, :, :A_NOPE] = nope
            o_ref[h, :, A_NOPE:] = rot.astype(o_ref.dtype)


def q_prep(cqn, wq, cos, sin, wuk_t=None):
    m, kq = cqn.shape
    tm = _tile(m, ROW_TILE, 8)
    wcols = wq.shape[2]
    hps = PREP_HEADS_PER_STEP
    assert A_HEADS % hps == 0
    absorbed = wuk_t is not None
    width = (KV_LORA if absorbed else A_NOPE) + A_ROPE
    in_specs = [pl.BlockSpec((tm, kq), lambda i, h: (i, 0)),
                pl.BlockSpec((hps, kq, wcols), lambda i, h: (h, 0, 0)),
                pl.BlockSpec((tm, LANES), lambda i, h: (i, 0)),
                pl.BlockSpec((tm, LANES), lambda i, h: (i, 0))]
    args = [cqn, wq, cos, sin]
    if absorbed:
        in_specs.append(pl.BlockSpec((hps, A_NOPE, KV_LORA), lambda i, h: (h, 0, 0)))
        args.append(wuk_t)
    return pl.pallas_call(
        functools.partial(_q_prep_kernel, absorbed=absorbed, hps=hps),
        grid=(m // tm, A_HEADS // hps),
        in_specs=in_specs,
        out_specs=pl.BlockSpec((hps, tm, width), lambda i, h: (h, i, 0)),
        out_shape=jax.ShapeDtypeStruct((A_HEADS, m, width), BF16),
        compiler_params=_params(2),
        name="q_prep",
    )(*args)


def _kv_prep_kernel(x_ref, wk_ref, wvt_ref, kr_ref, k_ref, vt_ref, *, blk, hps):
    x = x_ref[...]
    kr = kr_ref[...].astype(k_ref.dtype)
    for h in range(hps):
        k_ref[h, :, :A_NOPE] = _dot(x, wk_ref[h]).astype(k_ref.dtype)
        k_ref[h, :, A_NOPE:] = kr
        vt = _dot_nt(wvt_ref[h], x).astype(vt_ref.dtype)
        for c in range(x.shape[0] // blk):
            vt_ref[h, c, :A_VDIM, :] = vt[:, c * blk:(c + 1) * blk]
            vt_ref[h, c, A_VDIM:, :] = jnp.ones((A_VDIM, blk), vt_ref.dtype)


def kv_prep(ckv_b, krope, wk, wvt, blk):
    m = ckv_b.shape[0]
    tm = _tile(m, ROW_TILE, blk)
    hps = PREP_HEADS_PER_STEP
    assert A_HEADS % hps == 0
    return pl.pallas_call(
        functools.partial(_kv_prep_kernel, blk=blk, hps=hps),
        grid=(m // tm, A_HEADS // hps),
        in_specs=[pl.BlockSpec((tm, KV_LORA), lambda i, h: (i, 0)),
                  pl.BlockSpec((hps, KV_LORA, A_NOPE), lambda i, h: (h, 0, 0)),
                  pl.BlockSpec((hps, A_VDIM, KV_LORA), lambda i, h: (h, 0, 0)),
                  pl.BlockSpec((tm, A_ROPE), lambda i, h: (i, 0))],
        out_specs=[pl.BlockSpec((hps, tm, A_NOPE + A_ROPE), lambda i, h: (h, i, 0)),
                   pl.BlockSpec((hps, tm // blk, 2 * A_VDIM, blk), lambda i, h: (h, i, 0, 0))],
        out_shape=[jax.ShapeDtypeStruct((A_HEADS, m, A_NOPE + A_ROPE), BF16),
                   jax.ShapeDtypeStruct((A_HEADS, m // blk, 2 * A_VDIM, blk), BF16)],
        compiler_params=_params(2),
        name="kv_prep",
    )(ckv_b, wk, wvt, krope)


def _softmax_step(s, v, m, l, acc):
    m_new = jnp.maximum(m, jnp.max(s, axis=1, keepdims=True))
    p = jnp.exp(s - m_new)
    alpha = jnp.exp(m - m_new)
    l = alpha * l + jnp.sum(p, axis=1, keepdims=True)
    acc = alpha * acc + _dot(p.astype(BF16), v)
    return m_new, l, acc


def _flash_kernel(q_ref, k_ref, vt_ref, o_ref, m_ref, acc_ref, *, blk, hps):
    qi = pl.program_id(2)
    m_ref[...] = jnp.full_like(m_ref, -jnp.inf)
    acc_ref[...] = jnp.zeros_like(acc_ref)

    def block(kj, diagonal):
        start = pl.multiple_of(kj * blk, blk)
        scores = [_dot_nt(k_ref[h, pl.ds(start, blk), :], q_ref[h]) for h in range(hps)]
        if diagonal:
            key = lax.broadcasted_iota(jnp.int32, (blk, blk), 0)
            qry = lax.broadcasted_iota(jnp.int32, (blk, blk), 1)
            scores = [jnp.where(key <= qry, s, -jnp.inf) for s in scores]
        probs, alphas = [], []
        for h, s in enumerate(scores):
            m_old = m_ref[h]
            m_new = jnp.maximum(m_old, jnp.max(s, axis=0, keepdims=True))
            probs.append(jnp.exp(s - m_new).astype(BF16))
            alphas.append(jnp.exp(m_old - m_new))
            m_ref[h] = m_new
        for h in range(hps):
            acc_ref[h] = alphas[h] * acc_ref[h] + _dot(vt_ref[h, kj], probs[h])

    def body(kj, carry):
        block(kj, False)
        return carry

    lax.fori_loop(0, qi, body, 0)
    block(qi, True)
    for h in range(hps):
        acc = acc_ref[h]
        out_t = acc[:A_VDIM] / acc[A_VDIM:A_VDIM + 1]
        o_ref[:, h * A_VDIM:(h + 1) * A_VDIM] = out_t.T.astype(o_ref.dtype)


def flash_prompt(q, k, vt, batch, seq, blk):
    nq = seq // blk
    dqk = q.shape[2]
    hps = ATT_HEADS_PER_STEP
    assert A_HEADS % hps == 0
    return pl.pallas_call(
        functools.partial(_flash_kernel, blk=blk, hps=hps),
        grid=(batch, A_HEADS // hps, nq),
        in_specs=[pl.BlockSpec((hps, blk, dqk), lambda b, h, i: (h, b * nq + i, 0)),
                  pl.BlockSpec((hps, seq, dqk), lambda b, h, i: (h, b, 0)),
                  pl.BlockSpec((hps, nq, 2 * A_VDIM, blk), lambda b, h, i: (h, b, 0, 0))],
        out_specs=pl.BlockSpec((blk, hps * A_VDIM), lambda b, h, i: (b * nq + i, h)),
        out_shape=jax.ShapeDtypeStruct((batch * seq, A_HEADS * A_VDIM), BF16),
        scratch_shapes=[pltpu.VMEM((hps, 1, blk), F32), pltpu.VMEM((hps, 2 * A_VDIM, blk), F32)],
        compiler_params=_params(3),
        name="flash_prompt",
    )(q, k, vt)


def _paged_kernel(pt_ref, q_ref, ckv_new_ref, kr_new_ref, pool_ckv, pool_kr, o_ref,
                  ckv_buf, kr_buf, sem, m_ref, l_ref, acc_ref, *, n_steps, dec_seq):
    pp = PAGES_PER_STEP
    g = pl.program_id(1)
    step = pl.program_id(0) * n_steps + g
    last_step = pl.num_programs(0) * n_steps - 1
    slot = step % 2

    def page_copies(s, slot_idx):
        copies = []
        for i in range(pp):
            page = pt_ref[s * pp + i]
            copies.append(pltpu.make_async_copy(pool_ckv.at[page], ckv_buf.at[slot_idx, i], sem.at[slot_idx, 0]))
            copies.append(pltpu.make_async_copy(pool_kr.at[page], kr_buf.at[slot_idx, i], sem.at[slot_idx, 1]))
        return copies

    @pl.when(step == 0)
    def _():
        for c in page_copies(0, 0):
            c.start()

    nxt = jnp.minimum(step + 1, last_step)
    for c in page_copies(nxt, 1 - slot):
        c.start()
    for c in page_copies(step, slot):
        c.wait()
    ckv_refs = [ckv_buf.at[slot, i] for i in range(pp)]
    kr_refs = [kr_buf.at[slot, i] for i in range(pp)]

    @pl.when(g == 0)
    def _():
        m_ref[...] = jnp.full_like(m_ref, -jnp.inf)
        l_ref[...] = jnp.zeros_like(l_ref)
        acc_ref[...] = jnp.zeros_like(acc_ref)

    q = q_ref[...]
    q_lat, q_rope = q[:, :KV_LORA], q[:, KV_LORA:]
    grp = PAGES_PER_GROUP
    n_chain = pp // grp
    kcs, scores, probs, alphas = {}, {}, {}, {}

    def score(c):
        lo = c * grp
        kcs[c] = jnp.concatenate([r[...].astype(BF16) for r in ckv_refs[lo:lo + grp]], axis=0)
        krt = jnp.concatenate([r[...].astype(BF16) for r in kr_refs[lo:lo + grp]], axis=1)
        scores[c] = _dot_nt(q_lat, kcs[c]) + _dot(q_rope, krt)

    def stats(c):
        m_old = m_ref[c]
        m_new = jnp.maximum(m_old, jnp.max(scores[c], axis=1, keepdims=True))
        p = jnp.exp(scores[c] - m_new)
        alphas[c] = jnp.exp(m_old - m_new)
        l_ref[c] = alphas[c] * l_ref[c] + jnp.sum(p, axis=1, keepdims=True)
        m_ref[c] = m_new
        probs[c] = p.astype(BF16)

    def value(c):
        acc_ref[c] = alphas[c] * acc_ref[c] + _dot(probs[c], kcs[c])

    for c in range(n_chain + 2):
        if c < n_chain:
            score(c)
        if 0 <= c - 1 < n_chain:
            stats(c - 1)
        if 0 <= c - 2 < n_chain:
            value(c - 2)

    @pl.when(g == n_steps - 1)
    def _():
        def pad_rows(x):
            return jnp.concatenate([x, jnp.zeros((PAGE_SIZE - dec_seq, x.shape[1]), x.dtype)], axis=0)

        kcn = pad_rows(ckv_new_ref[...]).astype(BF16)
        krn = pad_rows(kr_new_ref[...]).astype(BF16)
        sn = _dot_nt(q_lat, kcn) + _dot_nt(q_rope, krn)
        tok = lax.broadcasted_iota(jnp.int32, sn.shape, 0) % dec_seq
        key = lax.broadcasted_iota(jnp.int32, sn.shape, 1)
        sn = jnp.where(key <= tok, sn, -jnp.inf)
        m, l, acc = _softmax_step(sn, kcn, m_ref[0], l_ref[0], acc_ref[0])
        for c in range(1, pp // grp):
            m_c = m_ref[c]
            m_new = jnp.maximum(m, m_c)
            a, b = jnp.exp(m - m_new), jnp.exp(m_c - m_new)
            l = a * l + b * l_ref[c]
            acc = a * acc + b * acc_ref[c]
            m = m_new
        o_ref[...] = (acc / l).astype(o_ref.dtype)

    @pl.when(step == last_step)
    def _():
        for c in page_copies(last_step, 1 - slot):
            c.wait()


def paged_attention(q, pool_ckv, pool_kr, page_table, ckv_new, kr_new):
    nb, rows, dq = q.shape
    n_pages = page_table.shape[1]
    dec_seq = ckv_new.shape[1]
    pp = PAGES_PER_STEP
    assert n_pages % pp == 0 and pp % PAGES_PER_GROUP == 0
    n_steps = n_pages // pp
    page = pool_ckv.shape[1]
    assert page == PAGE_SIZE and pool_kr.shape[1:] == (A_ROPE, page)

    in_specs = [pl.BlockSpec((None, rows, dq), lambda b, g, pt: (b, 0, 0)),
                pl.BlockSpec((None, dec_seq, KV_LORA), lambda b, g, pt: (b, 0, 0)),
                pl.BlockSpec((None, dec_seq, A_ROPE), lambda b, g, pt: (b, 0, 0)),
                pl.BlockSpec(memory_space=pl.ANY),
                pl.BlockSpec(memory_space=pl.ANY)]
    n_chain = pp // PAGES_PER_GROUP
    grid_spec = pltpu.PrefetchScalarGridSpec(
        num_scalar_prefetch=1,
        grid=(nb, n_steps),
        in_specs=in_specs,
        out_specs=pl.BlockSpec((None, rows, KV_LORA), lambda b, g, pt: (b, 0, 0)),
        scratch_shapes=[pltpu.VMEM((2, pp, page, KV_LORA), F32),
                        pltpu.VMEM((2, pp, A_ROPE, page), F32),
                        pltpu.SemaphoreType.DMA((2, 2)),
                        pltpu.VMEM((n_chain, rows, 1), F32),
                        pltpu.VMEM((n_chain, rows, 1), F32),
                        pltpu.VMEM((n_chain, rows, KV_LORA), F32)],
    )
    return pl.pallas_call(
        functools.partial(_paged_kernel, n_steps=n_steps, dec_seq=dec_seq),
        grid_spec=grid_spec,
        out_shape=jax.ShapeDtypeStruct((nb, rows, KV_LORA), BF16),
        compiler_params=_params(2),
        name="paged_attention",
    )(page_table.reshape(-1), q, ckv_new, kr_new, pool_ckv, pool_kr)


def _head_mm_kernel(x_ref, w_ref, o_ref):
    o_ref[...] = _dot(x_ref[...], w_ref[...]).astype(o_ref.dtype)


def head_matmul(x, w):
    nh, m, kd = x.shape
    n = w.shape[2]
    return pl.pallas_call(
        _head_mm_kernel,
        grid=(nh,),
        in_specs=[pl.BlockSpec((None, m, kd), lambda h: (h, 0, 0)),
                  pl.BlockSpec((None, kd, n), lambda h: (h, 0, 0))],
        out_specs=pl.BlockSpec((m, n), lambda h: (0, h)),
        out_shape=jax.ShapeDtypeStruct((m, nh * n), BF16),
        compiler_params=_params(1),
        name="head_matmul",
    )(x, w)


def _mlstm_kernel(q_ref, k_ref, v_ref, if_ref, mo_ref, bias_ref, gmh_ref, c0_ref, n0_ref, m0_ref,
                  o_ref, c_ref, n_ref, m_ref, *, rows, lp, nseq):
    chunk = pl.program_id(1)

    @pl.when(chunk == 0)
    def _():
        c_ref[...] = c0_ref[...]
        n_ref[...] = n0_ref[...]
        m_ref[...] = m0_ref[...]

    def pad(x):
        if rows == lp:
            return x
        return jnp.concatenate([x, jnp.zeros((lp - rows, x.shape[1]), x.dtype)], axis=0)

    t_idx = lax.broadcasted_iota(jnp.int32, (lp, lp), 0)
    s_idx = lax.broadcasted_iota(jnp.int32, (lp, lp), 1)
    causal = s_idx <= t_idx
    diag = s_idx == t_idx
    valid = lax.broadcasted_iota(jnp.int32, (lp, 1), 0) < rows
    for sq, h in [(sq, h) for sq in range(nseq) for h in range(M_HEADS)]:
        r0 = sq * rows
        gates = pad(if_ref[r0:r0 + rows, :] + bias_ref[...])
        q = pad(q_ref[r0:r0 + rows, h * M_DQK:(h + 1) * M_DQK]) * (M_DQK ** -0.5)
        k = pad(k_ref[r0:r0 + rows, h * M_DQK:(h + 1) * M_DQK])
        v = pad(v_ref[r0:r0 + rows, h * M_DV:(h + 1) * M_DV])
        i_col = gates[:, h:h + 1]
        f_pre = gates[:, M_HEADS + h:M_HEADS + h + 1]
        f_col = jnp.minimum(f_pre, 0.0) - jnp.log1p(jnp.exp(-jnp.abs(f_pre)))
        if rows != lp:
            i_col = jnp.where(valid, i_col, -jnp.inf)
            f_col = jnp.where(valid, f_col, 0.0)

        f_row = jnp.sum(jnp.where(diag, f_col, 0.0), axis=0, keepdims=True)
        i_row = jnp.sum(jnp.where(diag, i_col, 0.0), axis=0, keepdims=True)
        b_col = jnp.sum(jnp.where(causal, f_row, 0.0), axis=1, keepdims=True)
        b_row = jnp.sum(jnp.where(t_idx <= s_idx, f_col, 0.0), axis=0, keepdims=True)

        m_prev = m_ref[sq, h]
        c_prev = c_ref[sq, h]
        n_prev = n_ref[sq, h]
        a_col = b_col + m_prev
        dmat = jnp.where(causal, b_col - b_row + i_row, -jnp.inf)
        mt = jnp.maximum(a_col, jnp.max(dmat, axis=1, keepdims=True))
        w_inter = jnp.exp(a_col - mt)
        qb, kb, vb = q.astype(BF16), k.astype(BF16), v.astype(BF16)
        qk = _dot_nt(qb, kb) * jnp.exp(dmat - mt)
        num = _dot(qk.astype(BF16), vb) + w_inter * _dot(qb, c_prev.astype(BF16))
        den = jnp.sum(qk, axis=1, keepdims=True) + w_inter * jnp.sum(q * n_prev, axis=1, keepdims=True)
        hid = num / jnp.maximum(jnp.abs(den), jnp.exp(-mt))

        b_last = b_col[lp - 1:lp, :]
        g_col = b_last - b_col + i_col
        m_new = jnp.maximum(b_last + m_prev, jnp.max(g_col, axis=0, keepdims=True))
        decay = jnp.exp(b_last + m_prev - m_new)
        kw = jnp.exp(g_col - m_new) * k
        c_ref[sq, h] = decay * c_prev + _dot_tn(kw.astype(BF16), vb)
        n_ref[sq, h] = decay * n_prev + jnp.sum(kw, axis=0, keepdims=True)
        m_ref[sq, h] = m_new

        hn = hid * lax.rsqrt(jnp.mean(hid * hid, axis=1, keepdims=True) + EPS) * gmh_ref[h]
        out = jax.nn.sigmoid(pad(mo_ref[r0:r0 + rows, h * M_DV:(h + 1) * M_DV])) * hn
        o_ref[r0:r0 + rows, h * M_DV:(h + 1) * M_DV] = out[:rows].astype(o_ref.dtype)


def mlstm(proj, lay, bias_row, g_mh, c0, n0, m0, batch, seq):
    rows = _tile(seq, MLSTM_CHUNK, 8)
    lp = max(rows, MLSTM_MIN_ROWS)
    nc = seq // rows
    nseq = MLSTM_SEQS_PER_STEP if (nc == 1 and batch % MLSTM_SEQS_PER_STEP == 0) else 1
    tr = nseq * rows
    hq, hv = M_HEADS * M_DQK, M_HEADS * M_DV
    oq, ok, ov, oi, oo = (lay[n][0] for n in ("mq", "mk", "mv", "mif", "mo"))
    assert oq % hq == 0 and ok % hq == 0 and ov % hv == 0 and oo % hv == 0 and oi % LANES == 0

    def rowmap(col_block):
        return lambda b, c: (b * nc + c, col_block)

    def state_spec(*dims):
        return pl.BlockSpec((nseq, M_HEADS) + dims, lambda b, c: (b, 0, 0, 0))

    out, c1, n1, m1 = pl.pallas_call(
        functools.partial(_mlstm_kernel, rows=rows, lp=lp, nseq=nseq),
        grid=(batch // nseq, nc),
        in_specs=[pl.BlockSpec((tr, hq), rowmap(oq // hq)),
                  pl.BlockSpec((tr, hq), rowmap(ok // hq)),
                  pl.BlockSpec((tr, hv), rowmap(ov // hv)),
                  pl.BlockSpec((tr, LANES), rowmap(oi // LANES)),
                  pl.BlockSpec((tr, hv), rowmap(oo // hv)),
                  pl.BlockSpec((1, LANES), lambda b, c: (0, 0)),
                  pl.BlockSpec((M_HEADS, 1, M_DV), lambda b, c: (0, 0, 0)),
                  state_spec(M_DQK, M_DV), state_spec(1, M_DQK), state_spec(1, 1)],
        out_specs=[pl.BlockSpec((tr, hv), lambda b, c: (b * nc + c, 0)),
                   state_spec(M_DQK, M_DV), state_spec(1, M_DQK), state_spec(1, 1)],
        out_shape=[jax.ShapeDtypeStruct((batch * seq, hv), BF16),
                   jax.ShapeDtypeStruct((batch, M_HEADS, M_DQK, M_DV), F32),
                   jax.ShapeDtypeStruct((batch, M_HEADS, 1, M_DQK), F32),
                   jax.ShapeDtypeStruct((batch, M_HEADS, 1, 1), F32)],
        compiler_params=_params(2),
        name="mlstm",
    )(proj, proj, proj, proj, proj, bias_row, g_mh.reshape(M_HEADS, 1, M_DV),
      c0, n0.reshape(batch, M_HEADS, 1, M_DQK), m0.reshape(batch, M_HEADS, 1, 1))
    return out, c1, n1.reshape(batch, M_HEADS, M_DQK), m1.reshape(batch, M_HEADS)


def _ffn_up_kernel(x_ref, wg_ref, wv_ref, c0_ref, w_ref, b_ref, act_ref, tail_ref, wg_bf, wv_bf, *carry,
                   nb, tiles_per_seq):
    i = pl.program_id(1)
    _cast_weights_once([wg_ref, wv_ref], [wg_bf, wv_bf])
    tm, tc = act_ref.shape
    nsub = EPILOGUE_SUBTILES if (nb == 1 or nb % EPILOGUE_SUBTILES == 0) else 1
    rm = tm // nsub
    nbs, st = (1, rm) if nb == 1 else (nb // nsub, tm // nb)
    gates = [_dot(x_ref[s * rm:(s + 1) * rm, :], wg_bf[...]).reshape(nbs, st, tc)
             for s in range(nsub)]
    tails = [g[:, st - (CONV_W - 1):, :] for g in gates]
    last_rows = tails[-1] if nb == 1 else jnp.concatenate(tails, axis=0)
    prev = c0_ref[...]
    if tiles_per_seq > 1:
        carry_ref = carry[0]

        @pl.when(i == 0)
        def _():
            carry_ref[...] = jnp.zeros_like(carry_ref)

        prev = jnp.where(i % tiles_per_seq == 0, prev, carry_ref[...])
        carry_ref[...] = last_rows
    t = lax.broadcasted_iota(jnp.int32, (nbs, st, tc), 1)
    w = w_ref[...]
    for s in range(nsub):
        g = gates[s]
        if nb == 1:
            before = prev if s == 0 else tails[s - 1]
        else:
            before = prev[s * nbs:(s + 1) * nbs]
        prev1 = jnp.where(t == 0, before[:, 1:2, :], pltpu.roll(g, 1, 1))
        prev2 = jnp.where(t == 0, before[:, 0:1, :], jnp.where(t == 1, before[:, 1:2, :], pltpu.roll(g, 2, 1)))
        conv = b_ref[...] + w[0:1, :] * prev2 + w[1:2, :] * prev1 + w[2:3, :] * g
        val = _dot(x_ref[s * rm:(s + 1) * rm, :], wv_bf[...]).reshape(nbs, st, tc)
        act = jax.nn.gelu(conv, approximate=True) * val
        act_ref[s * rm:(s + 1) * rm, :] = act.reshape(rm, tc).astype(act_ref.dtype)
    tail_ref[...] = last_rows


def ffn_up_convglu(x, w_up, conv0, w_conv, b_conv, batch, seq):
    assert CONV_W == 3 and seq >= CONV_W - 1
    m, kd = x.shape
    tc = _tile(D_FF, 256)
    ncol = D_FF // tc
    if seq >= ROW_TILE:
        tm, nb = _tile(seq, ROW_TILE, 8), 1
    else:
        nb = max(1, min(batch, ROW_TILE // seq))
        assert batch % nb == 0
        tm = nb * seq
    tps = seq // (tm // nb)
    assert w_up.dtype == F32
    scratch = [pltpu.VMEM((kd, tc), BF16), pltpu.VMEM((kd, tc), BF16)]
    if tps > 1:
        scratch.append(pltpu.VMEM((nb, CONV_W - 1, tc), F32))
    act, tail = pl.pallas_call(
        functools.partial(_ffn_up_kernel, nb=nb, tiles_per_seq=tps),
        grid=(ncol, m // tm),
        in_specs=[pl.BlockSpec((tm, kd), lambda j, i: (i, 0)),
                  pl.BlockSpec((kd, tc), lambda j, i: (0, j)),
                  pl.BlockSpec((kd, tc), lambda j, i: (0, j + ncol)),
                  pl.BlockSpec((nb, CONV_W - 1, tc), lambda j, i: (i // tps, 0, j)),
                  pl.BlockSpec((CONV_W, tc), lambda j, i: (0, j)),
                  pl.BlockSpec((1, tc), lambda j, i: (0, j))],
        out_specs=[pl.BlockSpec((tm, tc), lambda j, i: (i, j)),
                   pl.BlockSpec((nb, CONV_W - 1, tc), lambda j, i: (i, 0, j))],
        out_shape=[jax.ShapeDtypeStruct((m, D_FF), BF16),
                   jax.ShapeDtypeStruct((batch * tps, CONV_W - 1, D_FF), F32)],
        scratch_shapes=scratch,
        compiler_params=_params(2),
        name="ffn_up_convglu",
    )(x, w_up, w_up, conv0, w_conv, b_conv.reshape(1, D_FF))
    return act, tail.reshape(batch, tps, CONV_W - 1, D_FF)[:, tps - 1]


_MODE_COPY, _MODE_DUP_ROPE, _MODE_GATES, _MODE_ZERO = 0, 1, 2, 3


def _relayout_kernel(off_ref, mode_ref, src_ref, o_ref):
    mode = mode_ref[pl.program_id(0)]
    row = lax.broadcasted_iota(jnp.int32, src_ref.shape, 0)

    @pl.when(mode == _MODE_COPY)
    def _():
        o_ref[...] = src_ref[...].T.astype(o_ref.dtype)

    @pl.when(mode == _MODE_DUP_ROPE)
    def _():
        kr = src_ref[:A_ROPE, :]
        o_ref[...] = jnp.concatenate([kr, kr], axis=0).T.astype(o_ref.dtype)

    @pl.when(mode == _MODE_GATES)
    def _():
        o_ref[...] = jnp.where(row < 2 * M_HEADS, src_ref[...], 0.0).T.astype(o_ref.dtype)

    @pl.when(mode == _MODE_ZERO)
    def _():
        o_ref[...] = jnp.zeros_like(o_ref)


def relayout_w_in(w_in_t, lay, total, src_off):
    assert 2 * A_ROPE == LANES and 2 * M_HEADS <= LANES
    n_tiles = total // LANES
    off, mode = np.zeros(n_tiles, np.int32), np.full(n_tiles, _MODE_ZERO, np.int32)
    groups = [(lay[n][0], lay[n][1], src_off[n], _MODE_COPY) for n in ("cq", "mq", "mk", "ckv", "mv", "mo", "ga", "gb")]
    groups += [(*lay["kr2"], src_off["kr"], _MODE_DUP_ROPE), (*lay["mif"], src_off["mi"], _MODE_GATES)]
    n_src, rows = w_in_t.shape
    for dst, width, s0, md in groups:
        assert dst % LANES == 0 and width % LANES == 0 and s0 % 8 == 0
        for t in range(width // LANES):
            off[dst // LANES + t], mode[dst // LANES + t] = s0 + t * LANES, md
    assert int(off.max()) + LANES <= n_src
    return pl.pallas_call(
        _relayout_kernel,
        grid_spec=pltpu.PrefetchScalarGridSpec(
            num_scalar_prefetch=2,
            grid=(n_tiles,),
            in_specs=[pl.BlockSpec((pl.Element(LANES), pl.Element(rows)),
                                   lambda t, off8, mode: (pl.multiple_of(off8[t] * 8, 8), 0))],
            out_specs=pl.BlockSpec((rows, LANES), lambda t, off, mode: (0, t)),
        ),
        out_shape=jax.ShapeDtypeStruct((rows, total), BF16),
        compiler_params=_params(1),
        name="relayout_w_in",
    )(jnp.asarray(off // 8), jnp.asarray(mode), w_in_t)


def _prepare_weights(w):
    lay, total = _padded_layout()
    offs = np.concatenate([[0], np.cumsum([Q_LORA, KV_LORA, A_ROPE, M_HEADS * M_DQK, M_HEADS * M_DQK,
                                            M_HEADS * M_DV, M_HEADS, M_HEADS, M_HEADS * M_DV, D_MODEL, D_MODEL])])
    src = {n: (int(offs[i]), int(offs[i + 1])) for i, n in enumerate(
        ["cq", "ckv", "kr", "mq", "mk", "mv", "mi", "mf", "mo", "ga", "gb"])}
    assert src["mf"][0] == src["mi"][1]
    w_in_p = relayout_w_in(jnp.swapaxes(w["w_in"], 0, 1), lay, total, {n: s[0] for n, s in src.items()})

    w_uq = w["w_uq"]
    rope_cols = w_uq[..., A_NOPE:]
    wq = jnp.concatenate([w_uq[..., :A_NOPE], rope_cols, rope_cols], axis=-1).transpose(1, 0, 2).astype(BF16)
    bias_row = jnp.concatenate([w["b_i"].astype(F32), w["b_f"].astype(F32),
                                jnp.zeros((LANES - 2 * M_HEADS,), F32)]).reshape(1, LANES)
    return {
        "lay": lay, "n_proj": total, "w_in_p": w_in_p, "wq": wq,
        "wuk_t": w["w_uk"].transpose(1, 2, 0).astype(BF16),
        "wk": w["w_uk"].transpose(1, 0, 2).astype(BF16),
        "wv": w["w_uv"].transpose(1, 0, 2).astype(BF16),
        "wvt": w["w_uv"].transpose(1, 2, 0).astype(BF16),
        "bias_row": bias_row,
        "w_br_a": w["w_br_a"].astype(F32), "w_br_b": w["w_br_b"].astype(F32),
        "w_o": w["w_o"].astype(F32), "w_up": w["w_up"].astype(F32), "w_down": w["w_down"].astype(BF16),
        "w_ple_gate": w["w_ple_gate"].astype(F32), "w_ple_proj": w["w_ple_proj"].astype(F32),
    }


def _rope_tables(pos, batch):
    half = A_ROPE // 2
    inv = jnp.power(ROPE_THETA, -jnp.arange(half, dtype=F32) / half)
    ang = pos[:, None] * inv[None, :]
    cos, sin = jnp.cos(ang), jnp.sin(ang)
    reps = LANES // A_ROPE
    cos_t = jnp.tile(jnp.concatenate([cos, cos], axis=1), (batch, reps))
    sin_t = jnp.tile(jnp.concatenate([-sin, sin], axis=1), (batch, reps))
    return cos_t, sin_t


def _hybrid_layer(x, pe, pos, w, pw, c0, n0, m0, conv0, paged):
    batch, seq, _ = x.shape
    m = batch * seq
    lay = pw["lay"]
    x2 = x.reshape(m, D_MODEL)
    cos, sin = _rope_tables(pos, batch)

    h = rmsnorm_rows(x2, w["g_mix"], BF16)
    proj = fused_matmul([(h, pw["w_in_p"])], [], _ep_plain, pw["n_proj"], F32,
                        tn=_tile(pw["n_proj"], 1024), name="in_proj")
    cqn, c_kv, ckv_b, k_rope = mla_prep(proj, lay, w["g_qa"], w["g_kva"], cos, sin)

    if paged is None:
        q = q_prep(cqn, pw["wq"], cos, sin)
        blk = _tile(seq, ATT_BLOCK)
        k, vt = kv_prep(ckv_b, k_rope, pw["wk"], pw["wvt"], blk)
        a_out = flash_prompt(q, k, vt, batch, seq, blk)
    else:
        pool_ckv, pool_kr, page_table = paged
        q = q_prep(cqn, pw["wq"], cos, sin, pw["wuk_t"])
        dq = q.shape[2]
        q = q.reshape(A_HEADS, batch, seq, dq).transpose(1, 0, 2, 3).reshape(batch, A_HEADS * seq, dq)
        o_lat = paged_attention(q, pool_ckv, pool_kr, page_table,
                                c_kv.reshape(batch, seq, KV_LORA), k_rope.reshape(batch, seq, A_ROPE))
        o_lat = o_lat.reshape(batch, A_HEADS, seq, KV_LORA).transpose(1, 0, 2, 3).reshape(A_HEADS, m, KV_LORA)
        a_out = head_matmul(o_lat, pw["wv"])

    b_out, c1, n1, m1 = mlstm(proj, lay, pw["bias_row"], w["g_mh"], c0, n0, m0, batch, seq)

    merged = fused_matmul([(a_out, pw["w_br_a"]), (b_out, pw["w_br_b"])],
                          [(proj, lay["ga"][0]), (proj, lay["gb"][0])],
                          _ep_gated_merge, D_MODEL, BF16, name="branch_merge")
    x2 = fused_matmul([(merged, pw["w_o"])], [(x2, 0)], _ep_residual, D_MODEL, F32, name="out_proj")

    hf = rmsnorm_rows(x2, w["g_ffn"], BF16)
    act, conv_tail = ffn_up_convglu(hf, pw["w_up"], conv0.astype(F32), w["w_conv"], w["b_conv"], batch, seq)
    x2 = fused_matmul([(act, pw["w_down"])], [(x2, 0)], _ep_residual, D_MODEL, F32,
                      tm=_tile(m, ROW_TILE // 2, 8), tn=_tile(D_MODEL, 512), name="ffn_down")

    hp = rmsnorm_rows(x2, w["g_ple"], BF16)
    x2 = fused_matmul([(hp, pw["w_ple_gate"]), (pe.reshape(m, PLE_DIM).astype(BF16), pw["w_ple_proj"])],
                      [(x2, 0)], _ep_ple, D_MODEL, F32, name="ple")
    return (x2, c_kv.reshape(batch, seq, KV_LORA), k_rope.reshape(batch, seq, A_ROPE), c1, n1, m1, conv_tail)


def kernel(x_prompt, x_sample, cache_ckv, cache_krope, state_C, state_n, state_m, state_conv, page_table,
           p_prompt, p_sample, g_mix, w_in, b_i, b_f, g_qa, w_uq, g_kva, w_uk, w_uv, g_mh, w_br_a, w_br_b,
           w_o, g_ffn, w_up, w_conv, b_conv, w_down, g_ple, w_ple_gate, w_ple_proj, g_final):
    bp, sp = x_prompt.shape[0], x_prompt.shape[1]
    bs, ss = x_sample.shape[0], x_sample.shape[1]
    depth = w_in.shape[0]
    past = page_table.shape[1] * cache_ckv.shape[2]
    pos_p = jnp.arange(sp, dtype=F32)
    pos_s = past + jnp.arange(ss, dtype=F32)
    c0p = jnp.zeros((bp, M_HEADS, M_DQK, M_DV), F32)
    n0p = jnp.zeros((bp, M_HEADS, M_DQK), F32)
    m0p = jnp.zeros((bp, M_HEADS), F32)
    conv0p = jnp.zeros((bp, CONV_W - 1, D_FF), F32)
    names = ["g_mix", "w_in", "b_i", "b_f", "g_qa", "w_uq", "g_kva", "w_uk", "w_uv", "g_mh", "w_br_a", "w_br_b",
             "w_o", "g_ffn", "w_up", "w_conv", "b_conv", "w_down", "g_ple", "w_ple_gate", "w_ple_proj"]
    stacked = [g_mix, w_in, b_i, b_f, g_qa, w_uq, g_kva, w_uk, w_uv, g_mh, w_br_a, w_br_b,
               w_o, g_ffn, w_up, w_conv, b_conv, w_down, g_ple, w_ple_gate, w_ple_proj]
    xp, xs = x_prompt, x_sample
    outs_p = [[] for _ in range(6)]
    outs_s = [[] for _ in range(6)]
    for l in range(depth):
        w = {n: a[l] for n, a in zip(names, stacked)}
        pw = _prepare_weights(w)
        xp2, *new_p = _hybrid_layer(xp, p_prompt[l], pos_p, w, pw, c0p, n0p, m0p, conv0p, None)
        xs2, *new_s = _hybrid_layer(xs, p_sample[l], pos_s, w, pw, state_C[l], state_n[l], state_m[l],
                                    state_conv[l],
                                    (cache_ckv[l], jnp.swapaxes(cache_krope[l], 1, 2), page_table))
        xp, xs = xp2.reshape(xp.shape), xs2.reshape(xs.shape)
        for j in range(6):
            outs_p[j].append(new_p[j])
            outs_s[j].append(new_s[j])
    stack_p = [jnp.stack(o) for o in outs_p]
    stack_s = [jnp.stack(o) for o in outs_s]
    y_prompt = rmsnorm_rows(xp.reshape(bp * sp, D_MODEL), g_final, F32).reshape(xp.shape)
    y_sample = rmsnorm_rows(xs.reshape(bs * ss, D_MODEL), g_final, F32).reshape(xs.shape)
    return (y_prompt, y_sample, *stack_p, *stack_s)
```
